```python
import math
import jax, jax.numpy as jnp
from jax import lax
import numpy as np

D_MODEL = 1024
BATCH = 8
SEQ = 4096
DEPTH = 2
DEC_BATCH = 128
DEC_SEQ = 1
PAST_LEN = 16384
PAGE_SIZE = 128

N_EVEN = (DEPTH + 1) // 2
N_ODD = DEPTH // 2
MLA_HEADS = 12
MLA_NOPE = 64
MLA_ROPE = 32
MLA_DV = 64
Q_LORA = 384
KV_LORA = 256
DIFF_HEADS = 4
DIFF_DH = 32
DIFF_DV = 2 * DIFF_DH
NSA_HEADS = 16
NSA_KV_HEADS = 2
NSA_GROUP = NSA_HEADS // NSA_KV_HEADS
NSA_DH = 64
NSA_KV_W = NSA_KV_HEADS * NSA_DH
CMP_STRIDE = 16
CMP_BLOCK = 2 * CMP_STRIDE
CMP_HID = 128
SEL_BLOCK = 64
N_SEL = 15
WINDOW = 512
NSA_QB = 64
MEM_HEADS = 4
MEM_DH = 128
MEM_W = MEM_HEADS * MEM_DH
D_FF = 2816
HALF_STEP = 0.5
ROPE_THETA = 10000.0
RMS_EPS = 1e-6
Q_BLOCK = 128
NEG = -1e30
TINY = 1e-30
EVEN_IN = Q_LORA + KV_LORA + MLA_ROPE + 2 * (DIFF_HEADS * 2 * DIFF_DH) + DIFF_HEADS * DIFF_DV
EVEN_OUT = MLA_HEADS * MLA_DV + DIFF_HEADS * DIFF_DV
ODD_OUT = NSA_HEADS * NSA_DH
ODD_IN = ODD_OUT + 6 * NSA_KV_W + 3 * NSA_HEADS
MLA_SCALE = (MLA_NOPE + MLA_ROPE) ** -0.5
DIFF_SCALE = DIFF_DH ** -0.5
NSA_SCALE = NSA_DH ** -0.5
MEM_SCALE = MEM_DH ** -0.5

kernel_name = "hybrid_mla_diff_nsa_decoder_step"


def rmsnorm(x, g):
    xf = x.astype(jnp.float32)
    y = xf * lax.rsqrt(jnp.mean(xf * xf, axis=-1, keepdims=True) + RMS_EPS)
    return (y * g.astype(jnp.float32)).astype(x.dtype)


def rope(x, pos):
    d = x.shape[-1]
    half = d // 2
    inv = ROPE_THETA ** (-jnp.arange(half, dtype=jnp.float32) * (2.0 / d))
    ang = pos.astype(jnp.float32)[:, None] * inv[None, :]
    shape = (ang.shape[0],) + (1,) * (x.ndim - 3) + (half,)
    cos = jnp.cos(ang).reshape(shape)
    sin = jnp.sin(ang).reshape(shape)
    xf = x.astype(jnp.float32)
    x1, x2 = xf[..., :half], xf[..., half:]
    return jnp.concatenate([x1 * cos - x2 * sin, x2 * cos + x1 * sin], axis=-1).astype(x.dtype)


def split_last(x, sizes):
    return jnp.split(x, [int(s) for s in np.cumsum(sizes)[:-1]], axis=-1)


def swiglu(h, w_gu, w_down):
    g, u = jnp.split(h @ w_gu, 2, axis=-1)
    return (jax.nn.silu(g) * u) @ w_down


def masked_softmax(s, mask):
    s = jnp.where(mask, s, NEG)
    e = jnp.where(mask, jnp.exp(s - s.max(-1, keepdims=True)), 0.0)
    return e / jnp.maximum(e.sum(-1, keepdims=True), TINY)


def causal_attention_blocks(q, k, v, scale):
    N, S, H, M, dk = q.shape
    nb = S // Q_BLOCK
    qb = jnp.moveaxis(q.reshape(N, nb, Q_BLOCK, H, M, dk), 1, 0)
    kpos = jnp.arange(S)

    def one(args):
        i, qi = args
        s = jnp.einsum('bqhmd,bkhmd->bhmqk', qi, k, preferred_element_type=jnp.float32) * scale
        qpos = i * Q_BLOCK + jnp.arange(Q_BLOCK)
        p = jax.nn.softmax(jnp.where(kpos[None, :] <= qpos[:, None], s, NEG), axis=-1)
        return jnp.einsum('bhmqk,bkhv->bqhmv', p, v)

    out = lax.map(one, (jnp.arange(nb), qb))
    return jnp.moveaxis(out, 0, 1).reshape(N, S, H, M, v.shape[-1]).astype(q.dtype)


def paged_online_attention(score_fn, value_fn, pools, li, page_table, new_rows):
    T = new_rows[0].shape[1]
    causal = jnp.tril(jnp.ones((T, T), dtype=bool))
    s = jnp.where(causal, score_fn(new_rows), NEG)
    m = s.max(-1)
    p = jnp.exp(s - m[..., None])
    init = (m, p.sum(-1), value_fn(p, new_rows))

    def step(carry, phys):
        m, l, acc = carry
        rows = tuple(pool[li, phys] for pool in pools)
        s = score_fn(rows)
        m_new = jnp.maximum(m, s.max(-1))
        corr = jnp.exp(m - m_new)
        p = jnp.exp(s - m_new[..., None])
        return (m_new, l * corr + p.sum(-1), acc * corr[..., None] + value_fn(p, rows)), None

    (m, l, acc), _ = lax.scan(step, init, page_table.T)
    return acc / l[..., None]


def diff_lambda_value(lam_p, lam_init):
    lp = lam_p.astype(jnp.float32)
    return jnp.exp(jnp.sum(lp[0] * lp[1])) - jnp.exp(jnp.sum(lp[2] * lp[3])) + lam_init


def even_project(h, pos, w_in, q_g, kv_g, w_uq):
    N, T, _ = h.shape
    dqk = DIFF_HEADS * 2 * DIFF_DH
    c_q, c_kv, k_rope, dq, dk, dv = split_last(h @ w_in, [Q_LORA, KV_LORA, MLA_ROPE, dqk, dqk, DIFF_HEADS * DIFF_DV])
    q = (rmsnorm(c_q, q_g) @ w_uq).reshape(N, T, MLA_HEADS, MLA_NOPE + MLA_ROPE)
    q_nope = q[..., :MLA_NOPE]
    q_rope = rope(q[..., MLA_NOPE:], pos)
    c_kv = rmsnorm(c_kv, kv_g)
    k_rope = rope(k_rope, pos)
    dq = rope(dq.reshape(N, T, DIFF_HEADS, 2, DIFF_DH), pos)
    dk = rope(dk.reshape(N, T, DIFF_HEADS, 2, DIFF_DH), pos)
    dv = dv.reshape(N, T, DIFF_HEADS, DIFF_DV)
    return q_nope, q_rope, c_kv, k_rope, dq, dk, dv


def even_finish(o_mla, o_diff, lam, lam_init, diff_g, w_out):
    N, T = o_mla.shape[:2]
    d = o_diff[..., 0, :].astype(jnp.float32) - lam * o_diff[..., 1, :].astype(jnp.float32)
    d = rmsnorm(d, diff_g) * (1.0 - lam_init)
    o = jnp.concatenate([o_mla.reshape(N, T, -1).astype(jnp.float32), d.reshape(N, T, -1)], axis=-1)
    return o.astype(w_out.dtype) @ w_out


def even_prompt(h, w_in, w_out, q_g, kv_g, w_uq, w_uk, w_uv, lam_p, diff_g, lam_init):
    N, S, _ = h.shape
    pos = jnp.arange(S)
    q_nope, q_rope, c_kv, k_rope, dq, dk, dv = even_project(h, pos, w_in, q_g, kv_g, w_uq)
    w_uk3 = w_uk.reshape(KV_LORA, MLA_HEADS, MLA_NOPE)
    w_uv3 = w_uv.reshape(KV_LORA, MLA_HEADS, MLA_DV)
    k_nope = jnp.einsum('nsc,chd->nshd', c_kv, w_uk3)
    v = jnp.einsum('nsc,chd->nshd', c_kv, w_uv3)
    q = jnp.concatenate([q_nope, q_rope], -1)[:, :, :, None]
    k = jnp.concatenate([k_nope, jnp.broadcast_to(k_rope[:, :, None, :], (N, S, MLA_HEADS, MLA_ROPE))], -1)[:, :, :, None]
    o_mla = causal_attention_blocks(q, k, v, MLA_SCALE)[:, :, :, 0]
    o_diff = causal_attention_blocks(dq, dk, dv, DIFF_SCALE)
    y = even_finish(o_mla, o_diff, diff_lambda_value(lam_p, lam_init), lam_init, diff_g, w_out)
    return y, (c_kv, k_rope, dk.reshape(N, S, DIFF_HEADS, 2 * DIFF_DH), dv)


def even_sample(h, li, cache_lat, cache_kr, cache_dk, cache_dv, page_table,
                w_in, w_out, q_g, kv_g, w_uq, w_uk, w_uv, lam_p, diff_g, lam_init):
    N, T, _ = h.shape
    past = page_table.shape[1] * cache_lat.shape[2]
    pos = past + jnp.arange(T)
    q_nope, q_rope, c_kv, k_rope, dq, dk, dv = even_project(h, pos, w_in, q_g, kv_g, w_uq)
    w_uk3 = w_uk.reshape(KV_LORA, MLA_HEADS, MLA_NOPE)
    w_uv3 = w_uv.reshape(KV_LORA, MLA_HEADS, MLA_DV)
    q_lat = jnp.einsum('nthd,chd->nthc', q_nope, w_uk3)

    def mla_score(rows):
        c, kr = rows
        return (jnp.einsum('nthc,npc->nhtp', q_lat, c, preferred_element_type=jnp.float32)
                + jnp.einsum('nthr,npr->nhtp', q_rope, kr, preferred_element_type=jnp.float32)) * MLA_SCALE

    def mla_value(p, rows):
        return jnp.einsum('nhtp,npc->nhtc', p, rows[0])

    lat = paged_online_attention(mla_score, mla_value, (cache_lat, cache_kr), li, page_table, (c_kv, k_rope))
    o_mla = jnp.einsum('nhtc,chd->nthd', lat, w_uv3)
    dk_flat = dk.reshape(N, T, DIFF_HEADS, 2 * DIFF_DH)

    def diff_score(rows):
        k = rows[0].reshape(rows[0].shape[:2] + (DIFF_HEADS, 2, DIFF_DH))
        return jnp.einsum('nthmc,nphmc->nhmtp', dq, k, preferred_element_type=jnp.float32) * DIFF_SCALE

    def diff_value(p, rows):
        return jnp.einsum('nhmtp,nphv->nhmtv', p, rows[1])

    od = paged_online_attention(diff_score, diff_value, (cache_dk, cache_dv), li, page_table, (dk_flat, dv))
    o_diff = od.transpose(0, 3, 1, 2, 4)
    y = even_finish(o_mla, o_diff, diff_lambda_value(lam_p, lam_init), lam_init, diff_g, w_out)
    return y, (c_kv, k_rope, dk_flat, dv)


def odd_project(h, pos, w_in):
    N, T, _ = h.shape
    q, ck, cv, sk, sv, wk, wv, gl = split_last(h @ w_in, [ODD_OUT] + [NSA_KV_W] * 6 + [3 * NSA_HEADS])
    kv = lambda z: z.reshape(N, T, NSA_KV_HEADS, NSA_DH)
    q = q.reshape(N, T, NSA_KV_HEADS, NSA_GROUP, NSA_DH)
    return dict(q=q, q_rot=rope(q, pos), ck=kv(ck), cv=kv(cv), sk=rope(kv(sk), pos), sv=kv(sv),
                wk=rope(kv(wk), pos), wv=kv(wv), gates=gl.reshape(N, T, NSA_KV_HEADS, NSA_GROUP, 3))


def chunk_proj(k, w1):
    N, L, G, d = k.shape
    nc = L // CMP_STRIDE
    ch = k[:, :nc * CMP_STRIDE].reshape(N, nc, CMP_STRIDE, G, d).transpose(0, 1, 3, 2, 4).reshape(N, nc, G, CMP_STRIDE * d)
    half = CMP_STRIDE * d
    return ch @ w1[:half], ch @ w1[half:]


def compress(a, b, pos_emb, w1, w2):
    pos_bias = pos_emb.reshape(-1) @ w1
    return jax.nn.silu(a[:, :-1] + b[:, 1:] + pos_bias) @ w2


def nsa_attend(q, q_rot, gates, qpos, kc, vc, gather_sel, wk, wv, wpos, nsb):
    N, T, G = q.shape[:3]
    NC = kc.shape[1]
    c_end = jnp.arange(NC) * CMP_STRIDE + (CMP_BLOCK - 1)
    s_c = jnp.einsum('ntgrd,ncgd->ngrtc', q, kc, preferred_element_type=jnp.float32) * NSA_SCALE
    p_c = masked_softmax(s_c, c_end[None, :] <= qpos[:, None])
    o_c = jnp.einsum('ngrtc,ncgd->ntgrd', p_c, vc)
    per = SEL_BLOCK // CMP_STRIDE
    imp = p_c.sum(2)
    imp = jnp.pad(imp, ((0, 0), (0, 0), (0, 0), (0, per * nsb - NC))).reshape(N, G, T, nsb, per)
    imp_blk = imp.sum(-1) + jnp.pad(imp[..., :-1, per - 1], ((0, 0), (0, 0), (0, 0), (1, 0)))
    cur = qpos // SEL_BLOCK
    imp_blk = jnp.where(jnp.arange(nsb)[None, :] < cur[:, None], imp_blk, NEG)
    _, top = lax.top_k(imp_blk, min(N_SEL, nsb))
    cur_b = jnp.broadcast_to(cur[None, None, :, None], (N, G, T, 1)).astype(top.dtype)
    idx = jnp.concatenate([top, cur_b], axis=-1)
    valid = jnp.concatenate([top < cur_b, jnp.ones((N, G, T, 1), dtype=bool)], axis=-1)
    ks, vs = gather_sel(idx)
    kpos = idx[..., None] * SEL_BLOCK + jnp.arange(SEL_BLOCK)
    mask_s = (valid[..., None] & (kpos <= qpos[None, None, :, None, None])).reshape(N, G, 1, T, -1)
    s_s = jnp.einsum('ntgrd,ngtsjd->ngrtsj', q_rot, ks, preferred_element_type=jnp.float32) * NSA_SCALE
    p_s = masked_softmax(s_s.reshape(s_s.shape[:4] + (-1,)), mask_s).reshape(s_s.shape)
    o_s = jnp.einsum('ngrtsj,ngtsjd->ntgrd', p_s, vs)
    s_w = jnp.einsum('ntgrd,nkgd->ngrtk', q_rot, wk, preferred_element_type=jnp.float32) * NSA_SCALE
    mask_w = ((wpos[None, :] <= qpos[:, None]) & (wpos[None, :] > qpos[:, None] - WINDOW) & (wpos[None, :] >= 0))
    o_w = jnp.einsum('ngrtk,nkgd->ntgrd', masked_softmax(s_w, mask_w), wv)
    g = jax.nn.sigmoid(gates.astype(jnp.float32))
    o = g[..., 0:1] * o_c + g[..., 1:2] * o_s + g[..., 2:3] * o_w
    return o.reshape(N, T, ODD_OUT)


def odd_prompt(h, w_in, w_out, cmp_pos, w1, w2):
    N, S, _ = h.shape
    G, d = NSA_KV_HEADS, NSA_DH
    pr = odd_project(h, jnp.arange(S), w_in)
    ak, bk = chunk_proj(pr['ck'], w1[0])
    av, bv = chunk_proj(pr['cv'], w1[1])
    kc = compress(ak, bk, cmp_pos[0], w1[0], w2[0])
    vc = compress(av, bv, cmp_pos[1], w1[1], w2[1])
    nsb = S // SEL_BLOCK
    kblk = pr['sk'].reshape(N, nsb, SEL_BLOCK, G, d).transpose(0, 3, 1, 2, 4)
    vblk = pr['sv'].reshape(N, nsb, SEL_BLOCK, G, d).transpose(0, 3, 1, 2, 4)
    bi = jnp.arange(N)[:, None, None, None]
    gi = jnp.arange(G)[None, :, None, None]

    def gather_sel(idx):
        return kblk[bi, gi, idx], vblk[bi, gi, idx]

    wk_pad = jnp.pad(pr['wk'], ((0, 0), (WINDOW, 0), (0, 0), (0, 0)))
    wv_pad = jnp.pad(pr['wv'], ((0, 0), (WINDOW, 0), (0, 0), (0, 0)))

    def one(i):
        q0 = i * NSA_QB
        sl = lambda z: lax.dynamic_slice_in_dim(z, q0, NSA_QB, axis=1)
        wk = lax.dynamic_slice_in_dim(wk_pad, q0, WINDOW + NSA_QB, axis=1)
        wv = lax.dynamic_slice_in_dim(wv_pad, q0, WINDOW + NSA_QB, axis=1)
        wpos = q0 - WINDOW + jnp.arange(WINDOW + NSA_QB)
        qpos = q0 + jnp.arange(NSA_QB)
        return nsa_attend(sl(pr['q']), sl(pr['q_rot']), sl(pr['gates']), qpos, kc, vc, gather_sel, wk, wv, wpos, nsb)

    out = lax.map(one, jnp.arange(S // NSA_QB))
    o = jnp.moveaxis(out, 0, 1).reshape(N, S, ODD_OUT)
    win = min(WINDOW, S)
    new = (pr['ck'], pr['cv'], pr['sk'], pr['sv'], pr['wk'][:, -win:], pr['wv'][:, -win:])
    return o.astype(w_out.dtype) @ w_out, new


def odd_sample(h, li, cache_ck, cache_cv, cache_sk, cache_sv, win_k_buf, win_v_buf, page_table,
               w_in, w_out, cmp_pos, w1, w2):
    N, T, _ = h.shape
    G, d = NSA_KV_HEADS, NSA_DH
    n_pages = page_table.shape[1]
    page = cache_ck.shape[2]
    past = n_pages * page
    pos = past + jnp.arange(T)
    pr = odd_project(h, pos, w_in)

    def page_chunks(phys):
        return chunk_proj(cache_ck[li, phys], w1[0]) + chunk_proj(cache_cv[li, phys], w1[1])

    pk_a, pk_b, pv_a, pv_b = lax.map(page_chunks, page_table.T)
    flat = lambda z: jnp.moveaxis(z, 0, 1).reshape(N, -1, G, CMP_HID)
    nk_a, nk_b = chunk_proj(pr['ck'], w1[0])
    nv_a, nv_b = chunk_proj(pr['cv'], w1[1])
    kc = compress(jnp.concatenate([flat(pk_a), nk_a], 1), jnp.concatenate([flat(pk_b), nk_b], 1), cmp_pos[0], w1[0], w2[0])
    vc = compress(jnp.concatenate([flat(pv_a), nv_a], 1), jnp.concatenate([flat(pv_b), nv_b], 1), cmp_pos[1], w1[1], w2[1])
    nsb = -(-(past + T) // SEL_BLOCK)
    n_new_blk = -(-T // SEL_BLOCK)
    padn = ((0, 0), (0, n_new_blk * SEL_BLOCK - T), (0, 0), (0, 0))
    sk_new = jnp.pad(pr['sk'], padn).reshape(N, n_new_blk, SEL_BLOCK, G, d).transpose(0, 3, 1, 2, 4)
    sv_new = jnp.pad(pr['sv'], padn).reshape(N, n_new_blk, SEL_BLOCK, G, d).transpose(0, 3, 1, 2, 4)
    di = jnp.arange(N)[:, None, None, None]
    gi4 = jnp.arange(G)[None, :, None, None]
    gi5 = jnp.arange(G)[None, :, None, None, None]

    def gather_sel(idx):
        start = idx * SEL_BLOCK
        in_past = (start < past)[..., None, None]
        phys = page_table[di, jnp.minimum(start // page, n_pages - 1)]
        row = (start % page)[..., None] + jnp.arange(SEL_BLOCK)
        kp = cache_sk[li, phys[..., None], row, gi5]
        vp = cache_sv[li, phys[..., None], row, gi5]
        loc = jnp.clip(idx - past // SEL_BLOCK, 0, n_new_blk - 1)
        return jnp.where(in_past, kp, sk_new[di, gi4, loc]), jnp.where(in_past, vp, sv_new[di, gi4, loc])

    wb = win_k_buf.shape[2]
    wk = jnp.concatenate([win_k_buf[li], pr['wk']], axis=1)
    wv = jnp.concatenate([win_v_buf[li], pr['wv']], axis=1)
    wpos = past - wb + jnp.arange(wb + T)
    o = nsa_attend(pr['q'], pr['q_rot'], pr['gates'], pos, kc, vc, gather_sel, wk, wv, wpos, nsb)
    new = (pr['ck'], pr['cv'], pr['sk'], pr['sv'], wk[:, -wb:], wv[:, -wb:])
    return o.astype(w_out.dtype) @ w_out, new


def mem_kv(mem, g, w_k, w_v):
    n, m, _ = mem.shape
    hm = rmsnorm(mem, g)
    return (hm @ w_k).reshape(n, m, MEM_HEADS, MEM_DH), (hm @ w_v).reshape(n, m, MEM_HEADS, MEM_DH)


def cross_attend(h, mk, mv, w_q, w_o):
    N, T, _ = h.shape
    q = (h @ w_q).reshape(N, T, MEM_HEADS, MEM_DH)
    s = jnp.einsum('nthd,nmhd->nhtm', q, mk, preferred_element_type=jnp.float32) * MEM_SCALE
    o = jnp.einsum('nhtm,nmhd->nthd', jax.nn.softmax(s, axis=-1), mv).reshape(N, T, MEM_W)
    return o.astype(h.dtype) @ w_o


def sublayers(x, gains, w_gu, w_dn, mixer, mk, mv, w_q, w_o):
    x = x + HALF_STEP * rmsnorm(swiglu(rmsnorm(x, gains[0]), w_gu[0], w_dn[0]), gains[1])
    y, new = mixer(rmsnorm(x, gains[2]))
    x = x + rmsnorm(y.astype(x.dtype), gains[3])
    x = x + rmsnorm(cross_attend(rmsnorm(x, gains[4]), mk, mv, w_q, w_o), gains[5])
    x = x + HALF_STEP * rmsnorm(swiglu(rmsnorm(x, gains[6]), w_gu[1], w_dn[1]), gains[7])
    return x, new


def setup_inputs(seed: int = 0) -> dict:
    key = jax.random.key(seed)
    ks = iter(jax.random.split(key, 64))
    nrm = lambda shape, scale: jax.random.normal(next(ks), shape, jnp.float32) * scale
    gain = lambda shape: 1.0 + 0.05 * jax.random.normal(next(ks), shape, jnp.float32)
    n_pages = PAST_LEN // PAGE_SIZE
    n_used = DEC_BATCH * n_pages
    n_pool = n_used + max(1, n_used // 4)
    win_buf = min(WINDOW, PAST_LEN)
    G, d = NSA_KV_HEADS, NSA_DH
    return {
        "x_prompt": nrm((BATCH, SEQ, D_MODEL), 1.0),
        "x_sample": nrm((DEC_BATCH, DEC_SEQ, D_MODEL), 1.0),
        "cache_mla_latent": nrm((N_EVEN, n_pool, PAGE_SIZE, KV_LORA), 1.0),
        "cache_mla_krope": nrm((N_EVEN, n_pool, PAGE_SIZE, MLA_ROPE), 1.0),
        "cache_diff_k": nrm((N_EVEN, n_pool, PAGE_SIZE, DIFF_HEADS, 2 * DIFF_DH), 1.0),
        "cache_diff_v": nrm((N_EVEN, n_pool, PAGE_SIZE, DIFF_HEADS, DIFF_DV), 1.0),
        "cache_nsa_cmp_k": nrm((N_ODD, n_pool, PAGE_SIZE, G, d), 1.0),
        "cache_nsa_cmp_v": nrm((N_ODD, n_pool, PAGE_SIZE, G, d), 1.0),
        "cache_nsa_sel_k": nrm((N_ODD, n_pool, PAGE_SIZE, G, d), 1.0),
        "cache_nsa_sel_v": nrm((N_ODD, n_pool, PAGE_SIZE, G, d), 1.0),
        "state_nsa_win_k": nrm((N_ODD, DEC_BATCH, win_buf, G, d), 1.0),
        "state_nsa_win_v": nrm((N_ODD, DEC_BATCH, win_buf, G, d), 1.0),
        "cache_mem_k": nrm((DEPTH, DEC_BATCH, 256, MEM_HEADS, MEM_DH), 1.0),
        "cache_mem_v": nrm((DEPTH, DEC_BATCH, 256, MEM_HEADS, MEM_DH), 1.0),
        "page_table": jax.random.permutation(next(ks), n_pool)[:n_used].reshape(DEC_BATCH, n_pages).astype(jnp.int32),
        "mem_prompt": nrm((BATCH, 256, D_MODEL), 1.0),
        "norm_g": gain((DEPTH, 8, D_MODEL)),
        "ffn_w_gu": nrm((DEPTH, 2, D_MODEL, 2 * D_FF), D_MODEL ** -0.5),
        "ffn_w_down": nrm((DEPTH, 2, D_FF, D_MODEL), D_FF ** -0.5),
        "mem_norm_g": gain((DEPTH, D_MODEL)),
        "mem_w_q": nrm((DEPTH, D_MODEL, MEM_W), D_MODEL ** -0.5),
        "mem_w_k": nrm((DEPTH, D_MODEL, MEM_W), D_MODEL ** -0.5),
        "mem_w_v": nrm((DEPTH, D_MODEL, MEM_W), D_MODEL ** -0.5),
        "mem_w_o": nrm((DEPTH, MEM_W, D_MODEL), MEM_W ** -0.5),
        "even_w_in": nrm((N_EVEN, D_MODEL, EVEN_IN), D_MODEL ** -0.5),
        "even_w_out": nrm((N_EVEN, EVEN_OUT, D_MODEL), EVEN_OUT ** -0.5),
        "mla_q_norm_g": gain((N_EVEN, Q_LORA)),
        "mla_kv_norm_g": gain((N_EVEN, KV_LORA)),
        "mla_w_uq": nrm((N_EVEN, Q_LORA, MLA_HEADS * (MLA_NOPE + MLA_ROPE)), Q_LORA ** -0.5),
        "mla_w_uk": nrm((N_EVEN, KV_LORA, MLA_HEADS * MLA_NOPE), KV_LORA ** -0.5),
        "mla_w_uv": nrm((N_EVEN, KV_LORA, MLA_HEADS * MLA_DV), KV_LORA ** -0.5),
        "diff_lambda": nrm((N_EVEN, 4, DIFF_DH), 0.1),
        "diff_norm_g": gain((N_EVEN, DIFF_DV)),
        "odd_w_in": nrm((N_ODD, D_MODEL, ODD_IN), D_MODEL ** -0.5),
        "odd_w_out": nrm((N_ODD, ODD_OUT, D_MODEL), ODD_OUT ** -0.5),
        "nsa_cmp_pos": nrm((N_ODD, 2, CMP_BLOCK, NSA_DH), 0.5),
        "nsa_cmp_w1": nrm((N_ODD, 2, CMP_BLOCK * NSA_DH, CMP_HID), (CMP_BLOCK * NSA_DH) ** -0.5),
        "nsa_cmp_w2": nrm((N_ODD, 2, CMP_HID, NSA_DH), CMP_HID ** -0.5),
    }


def reference(x_prompt, x_sample, cache_mla_latent, cache_mla_krope, cache_diff_k, cache_diff_v,
              cache_nsa_cmp_k, cache_nsa_cmp_v, cache_nsa_sel_k, cache_nsa_sel_v, state_nsa_win_k, state_nsa_win_v,
              cache_mem_k, cache_mem_v, page_table, mem_prompt, norm_g, ffn_w_gu, ffn_w_down, mem_norm_g,
              mem_w_q, mem_w_k, mem_w_v, mem_w_o, even_w_in, even_w_out, mla_q_norm_g, mla_kv_norm_g,
              mla_w_uq, mla_w_uk, mla_w_uv, diff_lambda, diff_norm_g, odd_w_in, odd_w_out,
              nsa_cmp_pos, nsa_cmp_w1, nsa_cmp_w2):
    xp, xs = x_prompt, x_sample
    even_p, even_s, odd_p, odd_s, mem_p = [], [], [], [], []
    for layer in range(DEPTH):
        li = layer // 2
        mk_p, mv_p = mem_kv(mem_prompt, mem_norm_g[layer], mem_w_k[layer], mem_w_v[layer])
        mem_p.append((mk_p, mv_p))
        if layer % 2 == 0:
            lam_init = 0.8 - 0.6 * math.exp(-0.3 * layer)
            ew = (even_w_in[li], even_w_out[li], mla_q_norm_g[li], mla_kv_norm_g[li], mla_w_uq[li],
                  mla_w_uk[li], mla_w_uv[li], diff_lambda[li], diff_norm_g[li], lam_init)
            mix_p = lambda h: even_prompt(h, *ew)
            mix_s = lambda h: even_sample(h, li, cache_mla_latent, cache_mla_krope, cache_diff_k, cache_diff_v,
                                          page_table, *ew)
        else:
            ow = (odd_w_in[li], odd_w_out[li], nsa_cmp_pos[li], nsa_cmp_w1[li], nsa_cmp_w2[li])
            mix_p = lambda h: odd_prompt(h, *ow)
            mix_s = lambda h: odd_sample(h, li, cache_nsa_cmp_k, cache_nsa_cmp_v, cache_nsa_sel_k, cache_nsa_sel_v,
                                         state_nsa_win_k, state_nsa_win_v, page_table, *ow)
        xp, new_p = sublayers(xp, norm_g[layer], ffn_w_gu[layer], ffn_w_down[layer], mix_p,
                              mk_p, mv_p, mem_w_q[layer], mem_w_o[layer])
        xs, new_s = sublayers(xs, norm_g[layer], ffn_w_gu[layer], ffn_w_down[layer], mix_s,
                              cache_mem_k[layer], cache_mem_v[layer], mem_w_q[layer], mem_w_o[layer])
        (even_p if layer % 2 == 0 else odd_p).append(new_p)
        (even_s if layer % 2 == 0 else odd_s).append(new_s)
    st = lambda lst, i: jnp.stack([e[i] for e in lst])
    return (xp, xs,
            st(even_p, 0), st(even_p, 1), st(even_p, 2), st(even_p, 3),
            st(odd_p, 0), st(odd_p, 1), st(odd_p, 2), st(odd_p, 3), st(odd_p, 4), st(odd_p, 5),
            st(mem_p, 0), st(mem_p, 1),
            st(even_s, 0), st(even_s, 1), st(even_s, 2), st(even_s, 3),
            st(odd_s, 0), st(odd_s, 1), st(odd_s, 2), st(odd_s, 3), st(odd_s, 4), st(odd_s, 5))
```

```python
import functools
import math

import jax
import jax.numpy as jnp
from jax import lax
from jax.experimental import pallas as pl
from jax.experimental.pallas import tpu as pltpu

BF = jnp.bfloat16
F32 = jnp.float32
SDS = jax.ShapeDtypeStruct

MLA_HEADS, MLA_NOPE, MLA_ROPE, MLA_DV = 12, 64, 32, 64
Q_LORA, KV_LORA = 384, 256
DIFF_HEADS, DIFF_DH, DIFF_DV = 4, 32, 64
NSA_HEADS, NSA_KV_HEADS, NSA_DH = 16, 2, 64
NSA_GROUP = NSA_HEADS // NSA_KV_HEADS
CMP_STRIDE, CMP_BLOCK, CMP_HID = 16, 32, 128
SEL_BLOCK, N_SEL, WINDOW = 64, 15, 512
MEM_HEADS, MEM_DH = 4, 128
HALF_STEP = 0.5
ROPE_THETA = 10000.0
RMS_EPS = 1e-6
NEG = -1e30
TINY = 1e-30
LOWEST = -3e38
MLA_SCALE = (MLA_NOPE + MLA_ROPE) ** -0.5
DIFF_SCALE = DIFF_DH ** -0.5
NSA_SCALE = NSA_DH ** -0.5
MEM_SCALE = MEM_DH ** -0.5
LANES = 128
DECODE_PAGES_PER_STEP = 8


def _dot(a, b):
    return jnp.dot(a, b, preferred_element_type=F32)


def _dot_nt(a, b):
    return lax.dot_general(a, b, (((1,), (1,)), ((), ())), preferred_element_type=F32)


def _dot_x3(x, w):
    h1 = x.astype(BF)
    r1 = x - h1.astype(F32)
    h2 = r1.astype(BF)
    h3 = (r1 - h2.astype(F32)).astype(BF)
    return _dot(h1, w) + _dot(h2, w) + _dot(h3, w)


def _rms(x, g):
    return x * lax.rsqrt(jnp.mean(x * x, axis=-1, keepdims=True) + RMS_EPS) * g


def _masked_softmax(s, mask):
    s = jnp.where(mask, s, NEG)
    e = jnp.where(mask, jnp.exp(s - s.max(-1, keepdims=True)), 0.0)
    return e / jnp.maximum(e.sum(-1, keepdims=True), TINY)


def _tile(m, pref):
    if m <= pref:
        return m
    t = pref
    while m % t:
        t //= 2
    return t


def _full(shape):
    n = len(shape)
    return pl.BlockSpec(shape, lambda *a: (0,) * n)


def _ffn_body(x_ref, gpre_ref, gpost_ref, wg_ref, wu_ref, wd_ref, o_ref, xn_ref, acc_ref, *, nf):
    j = pl.program_id(1)

    @pl.when(j == 0)
    def _():
        xn_ref[...] = _rms(x_ref[...], gpre_ref[...]).astype(BF)
        acc_ref[...] = jnp.zeros_like(acc_ref)

    xn = xn_ref[...]
    g = _dot(xn, wg_ref[...])
    u = _dot(xn, wu_ref[...])
    h = (jax.nn.silu(g) * u).astype(BF)
    acc_ref[...] += _dot(h, wd_ref[...])

    @pl.when(j == nf - 1)
    def _():
        o_ref[...] = x_ref[...] + HALF_STEP * _rms(acc_ref[...], gpost_ref[...])


def _ffn(x, gpre, gpost, w_gu, w_down):
    m, d = x.shape
    f = w_down.shape[0]
    tf = 256
    nf = f // tf
    tm = _tile(m, 512)
    return pl.pallas_call(
        functools.partial(_ffn_body, nf=nf),
        grid=(m // tm, nf),
        in_specs=[
            pl.BlockSpec((tm, d), lambda i, j: (i, 0)),
            pl.BlockSpec((1, d), lambda i, j: (0, 0)),
            pl.BlockSpec((1, d), lambda i, j: (0, 0)),
            pl.BlockSpec((d, tf), lambda i, j: (0, j)),
            pl.BlockSpec((d, tf), lambda i, j: (0, j + nf)),
            pl.BlockSpec((tf, d), lambda i, j: (j, 0)),
        ],
        out_specs=pl.BlockSpec((tm, d), lambda i, j: (i, 0)),
        out_shape=SDS((m, d), F32),
        scratch_shapes=[pltpu.VMEM((tm, d), BF), pltpu.VMEM((tm, d), F32)],
        name="ffn",
    )(x, gpre, gpost, w_gu, w_gu, w_down)


def _norm_matmul_body(x_ref, g_ref, w_ref, o_ref, xn_ref):
    @pl.when(pl.program_id(1) == 0)
    def _():
        xn_ref[...] = _rms(x_ref[...], g_ref[...]).astype(BF)

    o_ref[...] = _dot(xn_ref[...], w_ref[...]).astype(o_ref.dtype)


def _norm_matmul(x, g, w, out_dtype):
    m, d = x.shape
    n = w.shape[1]
    tm = _tile(m, 512)
    tn = _tile(n, 512)
    return pl.pallas_call(
        _norm_matmul_body,
        grid=(m // tm, n // tn),
        in_specs=[
            pl.BlockSpec((tm, d), lambda i, j: (i, 0)),
            pl.BlockSpec((1, d), lambda i, j: (0, 0)),
            pl.BlockSpec((d, tn), lambda i, j: (0, j)),
        ],
        out_specs=pl.BlockSpec((tm, tn), lambda i, j: (i, j)),
        out_shape=SDS((m, n), out_dtype),
        scratch_shapes=[pltpu.VMEM((tm, d), BF)],
        name="norm_matmul",
    )(x, g, w)


def _matmul_norm_res_body(a_ref, w_ref, x_ref, g_ref, o_ref):
    y = _dot(a_ref[...].astype(BF), w_ref[...])
    o_ref[...] = x_ref[...] + _rms(y, g_ref[...])


def _matmul_norm_res(a, w, x, g):
    m, k = a.shape
    d = w.shape[1]
    tm = _tile(m, 512)
    return pl.pallas_call(
        _matmul_norm_res_body,
        grid=(m // tm,),
        in_specs=[
            pl.BlockSpec((tm, k), lambda i: (i, 0)),
            pl.BlockSpec((k, d), lambda i: (0, 0)),
            pl.BlockSpec((tm, d), lambda i: (i, 0)),
            pl.BlockSpec((1, d), lambda i: (0, 0)),
        ],
        out_specs=pl.BlockSpec((tm, d), lambda i: (i, 0)),
        out_shape=SDS((m, d), F32),
        name="matmul_norm_res",
    )(a, w, x, g)


def _batched_matmul_body(a_ref, w_ref, o_ref):
    o_ref[0] = _dot(a_ref[0].astype(BF), w_ref[0]).astype(o_ref.dtype)


def _batched_matmul(a, w, out_dtype):
    b, m, k = a.shape
    n = w.shape[2]
    return pl.pallas_call(
        _batched_matmul_body,
        grid=(b,),
        in_specs=[pl.BlockSpec((1, m, k), lambda i: (i, 0, 0)), pl.BlockSpec((1, k, n), lambda i: (i, 0, 0))],
        out_specs=pl.BlockSpec((1, m, n), lambda i: (i, 0, 0)),
        out_shape=SDS((b, m, n), out_dtype),
        name="batched_matmul",
    )(a, w)


def _rope_tables(pos, d):
    half = d // 2
    inv = ROPE_THETA ** (-jnp.arange(half, dtype=F32) * (2.0 / d))
    ang = pos.astype(F32)[:, None] * inv[None, :]
    c, s = jnp.cos(ang), jnp.sin(ang)
    return jnp.concatenate([c, c], -1), jnp.concatenate([s, s], -1)


def _rot_cols(w, d):
    k, n = w.shape
    wr = w.reshape(k, n // d, 2, d // 2)
    return jnp.concatenate([-wr[:, :, 1], wr[:, :, 0]], axis=2).reshape(k, n)


def _pad_cols(w, n):
    return jnp.pad(w, ((0, 0), (0, n - w.shape[1])))


E_CQ, E_CKV, E_DV, E_DQ, E_DK, E_KR = 0, 384, 640, 896, 1152, 1408
E_DQR, E_DKR, E_KRR, E_W = 1536, 1792, 2048, 2176
QW = MLA_HEADS * LANES


def _even_weights(w_in, w_uq, w_uk, w_uv):
    dqk = DIFF_HEADS * 2 * DIFF_DH
    o = [0, Q_LORA, Q_LORA + KV_LORA, Q_LORA + KV_LORA + MLA_ROPE]
    c_q, c_kv, k_r = w_in[:, o[0]:o[1]], w_in[:, o[1]:o[2]], w_in[:, o[2]:o[3]]
    dq = w_in[:, o[3]:o[3] + dqk]
    dk = w_in[:, o[3] + dqk:o[3] + 2 * dqk]
    dv = w_in[:, o[3] + 2 * dqk:]
    w1 = jnp.concatenate([c_q, c_kv, dv, dq, dk, _pad_cols(k_r, LANES), _rot_cols(dq, DIFF_DH),
                          _rot_cols(dk, DIFF_DH), _pad_cols(_rot_cols(k_r, MLA_ROPE), LANES)], axis=1).astype(BF)
    uq = w_uq.reshape(Q_LORA, MLA_HEADS, MLA_NOPE + MLA_ROPE)
    z = jnp.zeros((Q_LORA, MLA_HEADS, LANES - MLA_NOPE - MLA_ROPE), w_uq.dtype)
    wq = jnp.concatenate([uq, z], axis=2).reshape(Q_LORA, QW).astype(BF)
    rot = _rot_cols(uq[:, :, MLA_NOPE:].reshape(Q_LORA, -1), MLA_ROPE).reshape(Q_LORA, MLA_HEADS, MLA_ROPE)
    wqr = jnp.concatenate([jnp.zeros_like(uq[:, :, :MLA_NOPE]), rot, z], axis=2).reshape(Q_LORA, QW).astype(BF)
    uk = w_uk.reshape(KV_LORA, MLA_HEADS, MLA_NOPE)
    wk_lat = jnp.concatenate([uk, jnp.zeros((KV_LORA, MLA_HEADS, LANES - MLA_NOPE), w_uk.dtype)], axis=2)
    wk_lat = wk_lat.reshape(KV_LORA, QW).astype(BF)
    place = jnp.zeros((LANES, LANES), F32).at[jnp.arange(MLA_ROPE), MLA_NOPE + jnp.arange(MLA_ROPE)].set(1.0)
    wk_rope = jnp.tile(place, (1, MLA_HEADS)).astype(BF)
    wabs = jnp.zeros((MLA_HEADS, LANES, KV_LORA + LANES), F32)
    wabs = wabs.at[:, :MLA_NOPE, :KV_LORA].set(jnp.transpose(uk, (1, 2, 0)))
    wabs = wabs.at[:, MLA_NOPE + jnp.arange(MLA_ROPE), KV_LORA + jnp.arange(MLA_ROPE)].set(1.0)
    wuv3 = jnp.transpose(w_uv.reshape(KV_LORA, MLA_HEADS, MLA_DV), (1, 0, 2)).astype(BF)
    return dict(w1=w1, wq=wq, wqr=wqr, wk_lat=wk_lat, wk_rope=wk_rope, wuv=w_uv.astype(BF),
                wabs=wabs.astype(BF), wuv3=wuv3)


def _even_tables(pos):
    c32, s32 = _rope_tables(pos, MLA_ROPE)
    t = pos.shape[0]
    cos256 = jnp.tile(c32, (1, 8))
    sin256 = jnp.tile(s32, (1, 8))
    cosq = jnp.concatenate([jnp.ones((t, MLA_NOPE), F32), c32, jnp.zeros((t, 32), F32)], axis=1)
    sinq = jnp.concatenate([jnp.zeros((t, MLA_NOPE), F32), s32, jnp.zeros((t, 32), F32)], axis=1)
    return cos256, sin256, cosq, sinq


def _even_proj_body(x_ref, g_ref, qg_ref, kvg_ref, w1_ref, wq_ref, wqr_ref, wkl_ref, wkr_ref, wuv_ref,
                    c256_ref, s256_ref, cq_ref, sq_ref,
                    q_out, k_out, v_out, ckv_out, kr_out, dq_out, dk_out, dkb_out, dv_out, dvb_out):
    xn = _rms(x_ref[...], g_ref[...]).astype(BF)
    proj = _dot(xn, w1_ref[...])
    c256, s256 = c256_ref[...], s256_ref[...]
    dq = proj[:, E_DQ:E_DQ + 256] * c256 + proj[:, E_DQR:E_DQR + 256] * s256
    dk = proj[:, E_DK:E_DK + 256] * c256 + proj[:, E_DKR:E_DKR + 256] * s256
    kr = proj[:, E_KR:E_KR + LANES] * c256[:, :LANES] + proj[:, E_KRR:E_KRR + LANES] * s256[:, :LANES]
    dv = proj[:, E_DV:E_DV + 256]
    cqn = _rms(proj[:, E_CQ:E_CQ + Q_LORA], qg_ref[...]).astype(BF)
    ckv = _rms(proj[:, E_CKV:E_CKV + KV_LORA], kvg_ref[...])
    ckv_b = ckv.astype(BF)
    q = _dot(cqn, wq_ref[...])
    qr = _dot(cqn, wqr_ref[...])
    cq, sq = cq_ref[...], sq_ref[...]
    for h in range(MLA_HEADS):
        sl = slice(h * LANES, (h + 1) * LANES)
        q_out[:, sl] = (q[:, sl] * cq + qr[:, sl] * sq).astype(BF)
    k_out[...] = (_dot(ckv_b, wkl_ref[...]) + _dot(kr.astype(BF), wkr_ref[...])).astype(BF)
    v_out[...] = _dot(ckv_b, wuv_ref[...]).astype(BF)
    ckv_out[...] = ckv
    kr_out[...] = kr
    dq_out[...] = dq.astype(BF)
    dk_out[...] = dk
    dkb_out[...] = dk.astype(BF)
    dv_out[...] = dv
    dvb_out[...] = dv.astype(BF)


def _even_proj(x, g, q_g, kv_g, ew, tables):
    m, d = x.shape
    tm = _tile(m, 256)
    nt = tables[0].shape[0] // tm
    row = lambda w: pl.BlockSpec((tm, w), lambda i: (i, 0))
    tab = lambda w: pl.BlockSpec((tm, w), lambda i: (i % nt, 0))
    vw = MLA_HEADS * MLA_DV
    return pl.pallas_call(
        _even_proj_body,
        grid=(m // tm,),
        in_specs=[row(d), _full((1, d)), _full((1, Q_LORA)), _full((1, KV_LORA)), _full((d, E_W)),
                  _full((Q_LORA, QW)), _full((Q_LORA, QW)), _full((KV_LORA, QW)), _full((LANES, QW)),
                  _full((KV_LORA, vw)), tab(256), tab(256), tab(LANES), tab(LANES)],
        out_specs=[row(QW), row(QW), row(vw), row(KV_LORA), row(LANES), row(256), row(256), row(256), row(256),
                   row(256)],
        out_shape=[SDS((m, QW), BF), SDS((m, QW), BF), SDS((m, vw), BF), SDS((m, KV_LORA), F32),
                   SDS((m, LANES), F32), SDS((m, 256), BF), SDS((m, 256), F32), SDS((m, 256), BF),
                   SDS((m, 256), F32), SDS((m, 256), BF)],
        name="even_proj",
    )(x, g, q_g, kv_g, ew["w1"], ew["wq"], ew["wqr"], ew["wk_lat"], ew["wk_rope"], ew["wuv"], *tables)


def _mla_prompt_body(q_ref, k_ref, v_ref, o_ref, m_ref, l_ref, acc_ref, *, tq):
    i = pl.program_id(2)
    j = pl.program_id(3)
    lo_half = lax.broadcasted_iota(jnp.int32, (tq, LANES), 1) < MLA_DV

    @pl.when(j == 0)
    def _():
        m_ref[...] = jnp.full_like(m_ref, NEG)
        l_ref[...] = jnp.zeros_like(l_ref)
        acc_ref[...] = jnp.zeros_like(acc_ref)

    def step(diag):
        v = v_ref[...]
        if diag:
            mask = (lax.broadcasted_iota(jnp.int32, (tq, tq), 1) <= lax.broadcasted_iota(jnp.int32, (tq, tq), 0))
        for a in range(2):
            sl = slice(a * LANES, (a + 1) * LANES)
            s = _dot_nt(q_ref[:, sl], k_ref[:, sl]) * MLA_SCALE
            if diag:
                s = jnp.where(mask, s, NEG)
            m_old = m_ref[a]
            m_new = jnp.maximum(m_old, s.max(-1, keepdims=True))
            corr = jnp.exp(m_old - m_new)
            p = jnp.exp(s - m_new)
            l_ref[a] = l_ref[a] * corr + p.sum(-1, keepdims=True)
            m_ref[a] = m_new
            pv = _dot(p.astype(BF), v)
            half = lo_half if a == 0 else jnp.logical_not(lo_half)
            acc_ref[...] = jnp.where(half, acc_ref[...] * corr + pv, acc_ref[...])

    @pl.when(j < i)
    def _():
        step(False)

    @pl.when(j == i)
    def _():
        step(True)
        o_ref[...] = (acc_ref[...] / jnp.where(lo_half, l_ref[0], l_ref[1])).astype(o_ref.dtype)


def _mla_prompt(q, k, v, n, s):
    tq = _tile(s, 512)
    nq = s // tq
    pairs = MLA_HEADS // 2
    return pl.pallas_call(
        functools.partial(_mla_prompt_body, tq=tq),
        grid=(n, pairs, nq, nq),
        in_specs=[
            pl.BlockSpec((tq, 2 * LANES), lambda b, p, i, j: (b * nq + i, p)),
            pl.BlockSpec((tq, 2 * LANES), lambda b, p, i, j: (b * nq + jnp.minimum(j, i), p)),
            pl.BlockSpec((tq, LANES), lambda b, p, i, j: (b * nq + jnp.minimum(j, i), p)),
        ],
        out_specs=pl.BlockSpec((tq, LANES), lambda b, p, i, j: (b * nq + i, p)),
        out_shape=SDS((n * s, MLA_HEADS * MLA_DV), BF),
        scratch_shapes=[pltpu.VMEM((2, tq, 1), F32), pltpu.VMEM((2, tq, 1), F32), pltpu.VMEM((tq, LANES), F32)],
        name="mla_prompt",
    )(q, k, v)


def _diff_prompt_body(q_ref, k_ref, v_ref, o0_ref, o1_ref, m_ref, l_ref, acc_ref, *, tq):
    i = pl.program_id(1)
    j = pl.program_id(2)
    lane = lax.broadcasted_iota(jnp.int32, (tq, 256), 1)
    nmaps = 2 * DIFF_HEADS

    @pl.when(j == 0)
    def _():
        m_ref[...] = jnp.full_like(m_ref, NEG)
        l_ref[...] = jnp.zeros_like(l_ref)
        acc_ref[...] = jnp.zeros_like(acc_ref)

    def step(diag):
        q, k, v = q_ref[...], k_ref[...], v_ref[...]
        if diag:
            mask = (lax.broadcasted_iota(jnp.int32, (tq, tq), 1) <= lax.broadcasted_iota(jnp.int32, (tq, tq), 0))
        for idx in range(nmaps):
            h, mm = idx // 2, idx % 2
            qm = jnp.where((lane >= idx * DIFF_DH) & (lane < (idx + 1) * DIFF_DH), q, jnp.zeros_like(q))
            s = _dot_nt(qm, k) * DIFF_SCALE
            if diag:
                s = jnp.where(mask, s, NEG)
            m_old = m_ref[idx]
            m_new = jnp.maximum(m_old, s.max(-1, keepdims=True))
            corr = jnp.exp(m_old - m_new)
            p = jnp.exp(s - m_new)
            l_ref[idx] = l_ref[idx] * corr + p.sum(-1, keepdims=True)
            m_ref[idx] = m_new
            pv = _dot(p.astype(BF), v)
            hmask = (lane >= h * DIFF_DV) & (lane < (h + 1) * DIFF_DV)
            acc_ref[mm] = jnp.where(hmask, acc_ref[mm] * corr + pv, acc_ref[mm])

    @pl.when(j < i)
    def _():
        step(False)

    @pl.when(j == i)
    def _():
        step(True)
        for mm, o_ref in ((0, o0_ref), (1, o1_ref)):
            lfull = jnp.zeros((tq, 256), F32)
            for h in range(DIFF_HEADS):
                hmask = (lane >= h * DIFF_DV) & (lane < (h + 1) * DIFF_DV)
                lfull = jnp.where(hmask, l_ref[2 * h + mm], lfull)
            o_ref[...] = acc_ref[mm] / lfull


def _diff_prompt(dq, dk, dv, n, s):
    tq = _tile(s, 256)
    nq = s // tq
    nmaps = 2 * DIFF_HEADS
    blk = lambda f: pl.BlockSpec((tq, 256), f)
    return pl.pallas_call(
        functools.partial(_diff_prompt_body, tq=tq),
        grid=(n, nq, nq),
        in_specs=[blk(lambda b, i, j: (b * nq + i, 0)), blk(lambda b, i, j: (b * nq + jnp.minimum(j, i), 0)),
                  blk(lambda b, i, j: (b * nq + jnp.minimum(j, i), 0))],
        out_specs=[blk(lambda b, i, j: (b * nq + i, 0)), blk(lambda b, i, j: (b * nq + i, 0))],
        out_shape=[SDS((n * s, 256), F32), SDS((n * s, 256), F32)],
        scratch_shapes=[pltpu.VMEM((nmaps, tq, 1), F32), pltpu.VMEM((nmaps, tq, 1), F32),
                        pltpu.VMEM((2, tq, 256), F32)],
        name="diff_prompt",
    )(dq, dk, dv)


def _even_finish_body(om_ref, o0_ref, o1_ref, lam_ref, dg_ref, wm_ref, wd_ref, x_ref, g_ref, o_ref, *, lam_init):
    lp = lam_ref[...]
    lam = (jnp.exp(jnp.sum(lp[0:1] * lp[1:2], axis=-1, keepdims=True))
           - jnp.exp(jnp.sum(lp[2:3] * lp[3:4], axis=-1, keepdims=True)) + lam_init)
    d = o0_ref[...] - lam * o1_ref[...]
    tm = d.shape[0]
    lane = lax.broadcasted_iota(jnp.int32, (tm, 256), 1)
    d2 = d * d
    inv = jnp.zeros_like(d)
    for h in range(DIFF_HEADS):
        hmask = (lane >= h * DIFF_DV) & (lane < (h + 1) * DIFF_DV)
        ms = jnp.sum(jnp.where(hmask, d2, 0.0), axis=-1, keepdims=True) * (1.0 / DIFF_DV)
        inv = jnp.where(hmask, lax.rsqrt(ms + RMS_EPS), inv)
    dn = (d * inv * dg_ref[...]) * (1.0 - lam_init)
    y = _dot(om_ref[...], wm_ref[...]) + _dot(dn.astype(BF), wd_ref[...])
    o_ref[...] = x_ref[...] + _rms(y, g_ref[...])


def _even_finish(o_mla, o0, o1, lam_p, diff_g256, w_out, x, g, lam_init):
    m, d = x.shape
    tm = _tile(m, 512)
    vw = MLA_HEADS * MLA_DV
    row = lambda w: pl.BlockSpec((tm, w), lambda i: (i, 0))
    return pl.pallas_call(
        functools.partial(_even_finish_body, lam_init=lam_init),
        grid=(m // tm,),
        in_specs=[row(vw), row(256), row(256), _full((4, DIFF_DH)), _full((1, 256)), _full((vw, d)),
                  _full((256, d)), row(d), _full((1, d))],
        out_specs=row(d),
        out_shape=SDS((m, d), F32),
        name="even_finish",
    )(o_mla, o0, o1, lam_p, diff_g256, w_out[:vw], w_out[vw:], x, g)


def _cross_body(q_ref, mk_ref, mv_ref, o_ref):
    for h in range(MEM_HEADS):
        sl = slice(h * MEM_DH, (h + 1) * MEM_DH)
        k = mk_ref[0, 0, :, h, :].astype(BF)
        v = mv_ref[0, 0, :, h, :].astype(BF)
        s = _dot_nt(q_ref[0, :, sl], k) * MEM_SCALE
        p = jax.nn.softmax(s, axis=-1)
        o_ref[0, :, sl] = _dot(p.astype(BF), v).astype(o_ref.dtype)


def _cross(q, mk, mv, layer):
    n, t, w = q.shape
    mem = mk.shape[2]
    tq = _tile(t, 512)
    mspec = pl.BlockSpec((1, 1, mem, MEM_HEADS, MEM_DH), lambda b, i: (layer, b, 0, 0, 0))
    return pl.pallas_call(
        _cross_body,
        grid=(n, t // tq),
        in_specs=[pl.BlockSpec((1, tq, w), lambda b, i: (b, i, 0)), mspec, mspec],
        out_specs=pl.BlockSpec((1, tq, w), lambda b, i: (b, i, 0)),
        out_shape=SDS((n, t, w), BF),
        name="cross",
    )(q, mk, mv)


def _even_decode_body(pt_ref, qd_ref, dq_ref, knew_ref, dkn_ref, dvn_ref, *rest, gp):
    lat_refs, kr_refs = rest[0:gp], rest[gp:2 * gp]
    dk_refs, dv_refs = rest[2 * gp:3 * gp], rest[3 * gp:4 * gp]
    lat_out, od_out, m1, l1, acc1, m2, l2, acc2 = rest[4 * gp:]
    step = pl.program_id(1)
    qd = qd_ref[0]
    dq = dq_ref[0]

    @pl.when(step == 0)
    def _():
        kn = knew_ref[0].astype(BF)
        s1 = jnp.sum(qd.astype(F32) * kn.astype(F32), axis=-1, keepdims=True) * MLA_SCALE
        m1[...] = s1
        l1[...] = jnp.ones_like(l1)
        acc1[...] = jnp.broadcast_to(kn[:, :KV_LORA].astype(F32), acc1.shape)
        s2 = jnp.sum(dq.astype(F32) * dkn_ref[0].astype(BF).astype(F32), axis=-1, keepdims=True) * DIFF_SCALE
        m2[...] = s2
        l2[...] = jnp.ones_like(l2)
        acc2[...] = dvn_ref[0].astype(BF).astype(F32)

    c = jnp.concatenate([r[0, 0] for r in lat_refs], axis=0).astype(BF)
    kr = jnp.concatenate([r[0, 0] for r in kr_refs], axis=0).astype(BF)
    s = (_dot_nt(qd[:, :KV_LORA], c) + _dot_nt(qd[:, KV_LORA:KV_LORA + MLA_ROPE], kr)) * MLA_SCALE
    m_new = jnp.maximum(m1[...], s.max(-1, keepdims=True))
    corr = jnp.exp(m1[...] - m_new)
    p = jnp.exp(s - m_new)
    l1[...] = l1[...] * corr + p.sum(-1, keepdims=True)
    acc1[...] = acc1[...] * corr + _dot(p.astype(BF), c)
    m1[...] = m_new

    page = dk_refs[0].shape[2]
    rows = page * DIFF_HEADS
    k2 = jnp.concatenate([r[0, 0].reshape(rows, 2 * DIFF_DH) for r in dk_refs], axis=0).astype(BF)
    v2 = jnp.concatenate([r[0, 0].reshape(rows, DIFF_DV) for r in dv_refs], axis=0).astype(BF)
    s = _dot_nt(dq, k2) * DIFF_SCALE
    nmaps = 2 * DIFF_HEADS
    col_h = lax.broadcasted_iota(jnp.int32, (nmaps, gp * rows), 1) % DIFF_HEADS
    row_h = lax.broadcasted_iota(jnp.int32, (nmaps, gp * rows), 0) // 2
    mask = col_h == row_h
    s = jnp.where(mask, s, NEG)
    m_new = jnp.maximum(m2[...], s.max(-1, keepdims=True))
    corr = jnp.exp(m2[...] - m_new)
    p = jnp.where(mask, jnp.exp(s - m_new), 0.0)
    l2[...] = l2[...] * corr + p.sum(-1, keepdims=True)
    acc2[...] = acc2[...] * corr + _dot(p.astype(BF), v2)
    m2[...] = m_new

    @pl.when(step == pl.num_programs(1) - 1)
    def _():
        lat_out[0] = acc1[...] / l1[...]
        od_out[0] = acc2[...] / l2[...]


def _paged_specs(shape_tail, li, gp):
    nd = len(shape_tail)
    specs = []
    for t in range(gp):
        specs.append(pl.BlockSpec((1, 1) + shape_tail,
                                  lambda n, s, pt, t=t: (li, pt[n, s * gp + t]) + (0,) * nd))
    return specs


def _even_decode(page_table, li, qd, dq8, knew, dkn, dvn, cache_lat, cache_kr, cache_dk, cache_dv):
    nseq, npages = page_table.shape
    gp = math.gcd(DECODE_PAGES_PER_STEP, npages)
    page = cache_lat.shape[2]
    hp = qd.shape[1]
    nmaps = 2 * DIFF_HEADS
    per_seq = lambda shp: pl.BlockSpec((1,) + shp, lambda n, s, pt: (n, 0, 0))
    in_specs = [per_seq((hp, KV_LORA + LANES)), per_seq((nmaps, 2 * DIFF_DH)), per_seq((1, KV_LORA + LANES)),
                per_seq((nmaps, 2 * DIFF_DH)), per_seq((nmaps, DIFF_DV))]
    in_specs += _paged_specs((page, KV_LORA), li, gp) + _paged_specs((page, MLA_ROPE), li, gp)
    in_specs += _paged_specs((page, DIFF_HEADS, 2 * DIFF_DH), li, gp)
    in_specs += _paged_specs((page, DIFF_HEADS, DIFF_DV), li, gp)
    return pl.pallas_call(
        functools.partial(_even_decode_body, gp=gp),
        grid_spec=pltpu.PrefetchScalarGridSpec(
            num_scalar_prefetch=1,
            grid=(nseq, npages // gp),
            in_specs=in_specs,
            out_specs=[per_seq((hp, KV_LORA)), per_seq((nmaps, DIFF_DV))],
            scratch_shapes=[pltpu.VMEM((hp, 1), F32), pltpu.VMEM((hp, 1), F32), pltpu.VMEM((hp, KV_LORA), F32),
                            pltpu.VMEM((nmaps, 1), F32), pltpu.VMEM((nmaps, 1), F32),
                            pltpu.VMEM((nmaps, DIFF_DV), F32)],
        ),
        out_shape=[SDS((nseq, hp, KV_LORA), F32), SDS((nseq, nmaps, DIFF_DV), F32)],
        name="even_decode",
    )(page_table, qd, dq8, knew, dkn, dvn, *([cache_lat] * gp), *([cache_kr] * gp), *([cache_dk] * gp),
      *([cache_dv] * gp))


O_Q, O_QR, O_CK, O_CV, O_SK, O_SKR, O_SV, O_WK, O_WKR, O_WV, O_GT = (
    0, 1024, 2048, 2176, 2304, 2432, 2560, 2688, 2816, 2944, 3072)
O_SKD, O_SKDR, O_SVD, O_WKD, O_WKDR, O_WVD, O_W = 3200, 3456, 3712, 3968, 4224, 4480, 4736
KVW = NSA_KV_HEADS * NSA_DH


def _dup_cols(w):
    k = w.shape[0]
    wr = w.reshape(k, NSA_KV_HEADS, 1, NSA_DH)
    return jnp.broadcast_to(wr, (k, NSA_KV_HEADS, 2, NSA_DH)).reshape(k, 2 * KVW)


def _odd_weights(w_in):
    hw = NSA_HEADS * NSA_DH
    q = w_in[:, :hw]
    ck, cv, sk, sv, wk, wv = [w_in[:, hw + t * KVW: hw + (t + 1) * KVW] for t in range(6)]
    gl = w_in[:, hw + 6 * KVW:]
    skr, wkr = _rot_cols(sk, NSA_DH), _rot_cols(wk, NSA_DH)
    return jnp.concatenate([q, _rot_cols(q, NSA_DH), ck, cv, sk, skr, sv, wk, wkr, wv, _pad_cols(gl, LANES),
                            _dup_cols(sk), _dup_cols(skr), _dup_cols(sv), _dup_cols(wk), _dup_cols(wkr),
                            _dup_cols(wv)], axis=1).astype(BF)


def _odd_proj_body(x_ref, g_ref, w_ref, cos_ref, sin_ref,
                   q_out, qr_out, ck_out, cv_out, sk_out, sv_out, wk_out, wv_out, gt_out,
                   skd_out, svd_out, wkd_out, wvd_out):
    xn = _rms(x_ref[...], g_ref[...]).astype(BF)
    proj = _dot(xn, w_ref[...])
    cos, sin = cos_ref[...], sin_ref[...]
    for c in range(NSA_HEADS * NSA_DH // LANES):
        sl = slice(c * LANES, (c + 1) * LANES)
        q = proj[:, O_Q + c * LANES:O_Q + (c + 1) * LANES]
        q_out[:, sl] = q.astype(BF)
        qr_out[:, sl] = (q * cos + proj[:, O_QR + c * LANES:O_QR + (c + 1) * LANES] * sin).astype(BF)
    ck_out[...] = proj[:, O_CK:O_CK + KVW]
    cv_out[...] = proj[:, O_CV:O_CV + KVW]
    sk_out[...] = proj[:, O_SK:O_SK + KVW] * cos + proj[:, O_SKR:O_SKR + KVW] * sin
    sv_out[...] = proj[:, O_SV:O_SV + KVW]
    wk_out[...] = proj[:, O_WK:O_WK + KVW] * cos + proj[:, O_WKR:O_WKR + KVW] * sin
    wv_out[...] = proj[:, O_WV:O_WV + KVW]
    gt_out[...] = proj[:, O_GT:O_GT + LANES]
    for g in range(NSA_KV_HEADS):
        sl = slice(g * LANES, (g + 1) * LANES)
        skd_out[:, sl] = (proj[:, O_SKD + g * LANES:O_SKD + (g + 1) * LANES] * cos
                          + proj[:, O_SKDR + g * LANES:O_SKDR + (g + 1) * LANES] * sin).astype(BF)
        wkd_out[:, sl] = (proj[:, O_WKD + g * LANES:O_WKD + (g + 1) * LANES] * cos
                          + proj[:, O_WKDR + g * LANES:O_WKDR + (g + 1) * LANES] * sin).astype(BF)
    svd_out[...] = proj[:, O_SVD:O_SVD + 2 * KVW].astype(BF)
    wvd_out[...] = proj[:, O_WVD:O_WVD + 2 * KVW].astype(BF)


def _odd_proj(x, g, w1, cos, sin):
    m, d = x.shape
    tm = _tile(m, 256)
    nt = cos.shape[0] // tm
    hw = NSA_HEADS * NSA_DH
    row = lambda w: pl.BlockSpec((tm, w), lambda i: (i, 0))
    tab = pl.BlockSpec((tm, LANES), lambda i: (i % nt, 0))
    return pl.pallas_call(
        _odd_proj_body,
        grid=(m // tm,),
        in_specs=[row(d), _full((1, d)), _full((d, O_W)), tab, tab],
        out_specs=[row(hw), row(hw)] + [row(KVW)] * 6 + [row(LANES)] + [row(2 * KVW)] * 4,
        out_shape=[SDS((m, hw), BF), SDS((m, hw), BF)] + [SDS((m, KVW), F32)] * 6 + [SDS((m, LANES), F32)]
        + [SDS((m, 2 * KVW), BF)] * 4,
        name="odd_proj",
    )(x, g, w1, cos, sin)


def _cmp_mlp_body(ch_ref, w1_ref, bias_ref, w2_ref, o_ref):
    ab = _dot(ch_ref[0, 0], w1_ref[0])
    nc = ab.shape[0]
    a = ab[:, :CMP_HID]
    b = pltpu.roll(ab[:, CMP_HID:], nc - 1, 0)
    hid = jax.nn.silu(a + b + bias_ref[0])
    o_ref[0, 0] = _dot(hid.astype(BF), w2_ref[0]).astype(o_ref.dtype)


def _cmp_mlp(ch, w1ab, bias, w2dup):
    _, b, nc, kk = ch.shape
    return pl.pallas_call(
        _cmp_mlp_body,
        grid=(2, b),
        in_specs=[pl.BlockSpec((1, 1, nc, kk), lambda t, i: (t, i, 0, 0)),
                  pl.BlockSpec((1, kk, 2 * CMP_HID), lambda t, i: (t, 0, 0)),
                  pl.BlockSpec((1, 1, CMP_HID), lambda t, i: (t, 0, 0)),
                  pl.BlockSpec((1, CMP_HID, LANES), lambda t, i: (t, 0, 0))],
        out_specs=pl.BlockSpec((1, 1, nc, LANES), lambda t, i: (t, i, 0, 0)),
        out_shape=SDS((2, b, nc, LANES), BF),
        name="cmp_mlp",
    )(ch, w1ab, bias, w2dup)


def _topk_pick(vals, k, width):
    rows = vals.shape[0]
    lane_f = lax.broadcasted_iota(jnp.int32, (rows, width), 1).astype(F32)
    picked = jnp.zeros((rows, width), jnp.bool_)
    idxs = []
    for _ in range(k):
        mx = vals.max(-1, keepdims=True)
        idx = jnp.min(jnp.where(vals == mx, lane_f, float(width)), axis=-1, keepdims=True)
        hit = lane_f == idx
        picked = jnp.logical_or(picked, hit)
        vals = jnp.where(hit, LOWEST, vals)
        idxs.append(idx)
    return picked, idxs


def _nsa_prompt_body(q_ref, qr_ref, gt_ref, kc_ref, vc_ref, ks_ref, vs_ref, kw_ref, vw_ref, e_ref, mb_ref, o_ref,
                     res_ref, m_ref, l_ref, acc_ref, *, tq, nc, nsb):
    i = pl.program_id(1)
    q0 = i * tq
    qpos = q0 + lax.broadcasted_iota(jnp.int32, (tq, 1), 0)
    lane = lax.broadcasted_iota(jnp.int32, (tq, LANES), 1)
    lo_half = lane < NSA_DH
    gates = jax.nn.sigmoid(gt_ref[...])
    c_end = lax.broadcasted_iota(jnp.int32, (tq, nc), 1) * CMP_STRIDE + (CMP_BLOCK - 1)
    mask_c = c_end <= qpos
    zero_b = jnp.zeros((tq, LANES), BF)

    def head_q(ref, hd):
        qh = ref[:, (hd // 2) * LANES:(hd // 2 + 1) * LANES]
        return jnp.where(lo_half if hd % 2 == 0 else jnp.logical_not(lo_half), qh, zero_b)

    def online(r, qm, kt, vt, mask):
        s = jnp.where(mask, _dot_nt(qm, kt) * NSA_SCALE, NEG)
        m_old = m_ref[r]
        m_new = jnp.maximum(m_old, s.max(-1, keepdims=True))
        corr = jnp.exp(m_old - m_new)
        p = jnp.where(mask, jnp.exp(s - m_new), 0.0)
        l_ref[r] = l_ref[r] * corr + p.sum(-1, keepdims=True)
        acc_ref[r] = acc_ref[r] * corr + _dot(p.astype(BF), vt)
        m_ref[r] = m_new

    def reset():
        m_ref[...] = jnp.full_like(m_ref, NEG)
        l_ref[...] = jnp.zeros_like(l_ref)
        acc_ref[...] = jnp.zeros_like(acc_ref)

    for g in range(NSA_KV_HEADS):
        gsl = slice(g * LANES, (g + 1) * LANES)
        heads = [g * NSA_GROUP + r for r in range(NSA_GROUP)]
        kc, vc = kc_ref[0, g], vc_ref[0, g]
        imp = jnp.zeros((tq, nc), F32)
        for hd in heads:
            p = _masked_softmax(_dot_nt(head_q(q_ref, hd), kc) * NSA_SCALE, mask_c)
            imp = imp + p
            res_ref[hd] = gates[:, 3 * hd:3 * hd + 1] * _dot(p.astype(BF), vc)
        imp_blk = _dot_x3(imp, mb_ref[...])
        cur = qpos // SEL_BLOCK
        vals = jnp.where(lane < cur, imp_blk, NEG)
        vals = jnp.where(lane < nsb, vals, LOWEST)
        picked, _ = _topk_pick(vals, min(N_SEL, nsb), LANES)
        sel = jnp.logical_or(jnp.logical_and(picked, lane < cur), lane == cur)
        selb = jnp.where(sel, 1.0, 0.0).astype(BF)
        qrs = [head_q(qr_ref, hd) for hd in heads]

        reset()

        def sel_tile(j, carry):
            k0 = pl.multiple_of(j * tq, tq)
            kt = ks_ref[pl.ds(k0, tq), gsl]
            vt = vs_ref[pl.ds(k0, tq), gsl]
            kpos = k0 + lax.broadcasted_iota(jnp.int32, (1, tq), 1)
            mask = jnp.logical_and(_dot(selb, e_ref[:, pl.ds(k0, tq)]) > 0.5, kpos <= qpos)
            for r in range(NSA_GROUP):
                online(r, qrs[r], kt, vt, mask)
            return carry

        lax.fori_loop(0, i + 1, sel_tile, 0)
        for r, hd in enumerate(heads):
            res_ref[hd] += gates[:, 3 * hd + 1:3 * hd + 2] * (acc_ref[r] / jnp.maximum(l_ref[r], TINY))

        reset()

        def win_tile(j, carry):
            k0 = pl.multiple_of(j * tq, tq)
            kt = kw_ref[pl.ds(k0, tq), gsl]
            vt = vw_ref[pl.ds(k0, tq), gsl]
            kpos = k0 + lax.broadcasted_iota(jnp.int32, (1, tq), 1)
            mask = jnp.logical_and(kpos <= qpos, kpos > qpos - WINDOW)
            for r in range(NSA_GROUP):
                online(r, qrs[r], kt, vt, mask)
            return carry

        lax.fori_loop(jnp.maximum(q0 - (WINDOW - 1), 0) // tq, i + 1, win_tile, 0)
        for r, hd in enumerate(heads):
            res_ref[hd] += gates[:, 3 * hd + 2:3 * hd + 3] * (acc_ref[r] / jnp.maximum(l_ref[r], TINY))

    for pb in range(NSA_HEADS // 2):
        o_ref[:, pb * LANES:(pb + 1) * LANES] = jnp.where(lo_half, res_ref[2 * pb], res_ref[2 * pb + 1]).astype(
            o_ref.dtype)


def _nsa_tables(s, nc, nsb):
    c = jnp.arange(nc)[:, None]
    j = jnp.arange(LANES)[None, :]
    per = SEL_BLOCK // CMP_STRIDE
    mb = ((c >= per * j - 1) & (c <= per * j + per - 1) & (c < nc - 1) & (j < nsb)).astype(BF)
    e = (jnp.arange(s)[None, :] // SEL_BLOCK == jnp.arange(LANES)[:, None]).astype(BF)
    return mb, e


def _nsa_prompt(q, qr, gt, cmp, skd, svd, wkd, wvd, n, s):
    tq = _tile(s, 256)
    nc = s // CMP_STRIDE
    nsb = s // SEL_BLOCK
    assert nsb <= LANES
    hw = NSA_HEADS * NSA_DH
    mb, e = _nsa_tables(s, nc, nsb)
    nq = s // tq
    row = lambda w: pl.BlockSpec((tq, w), lambda b, i: (b * nq + i, 0))
    seq = pl.BlockSpec((s, 2 * KVW), lambda b, i: (b, 0))
    return pl.pallas_call(
        functools.partial(_nsa_prompt_body, tq=tq, nc=nc, nsb=nsb),
        grid=(n, nq),
        in_specs=[row(hw), row(hw), row(LANES),
                  pl.BlockSpec((1, NSA_KV_HEADS, nc, LANES), lambda b, i: (0, b, 0, 0)),
                  pl.BlockSpec((1, NSA_KV_HEADS, nc, LANES), lambda b, i: (1, b, 0, 0)),
                  seq, seq, seq, seq, _full((LANES, s)), _full((nc, LANES))],
        out_specs=row(hw),
        out_shape=SDS((n * s, hw), BF),
        scratch_shapes=[pltpu.VMEM((NSA_HEADS, tq, LANES), F32), pltpu.VMEM((NSA_GROUP, tq, 1), F32),
                        pltpu.VMEM((NSA_GROUP, tq, 1), F32), pltpu.VMEM((NSA_GROUP, tq, LANES), F32)],
        name="nsa_prompt",
    )(q, qr, gt, cmp, cmp, skd, svd, wkd, wvd, e, mb)


def _nsa_cmp_decode_body(pt_ref, q_ref, wabk_ref, wabv_ref, bias_ref, w2_ref, *rest, gp, nchunks, qpos):
    ck_refs, cv_refs = rest[0:gp], rest[gp:2 * gp]
    oc_out, imp_out, abk, abv = rest[2 * gp:]
    step = pl.program_id(1)
    page = ck_refs[0].shape[2]
    cpp = page // CMP_STRIDE
    rows = gp * cpp * NSA_KV_HEADS
    for refs, wab, ab in ((ck_refs, wabk_ref, abk), (cv_refs, wabv_ref, abv)):
        tot = jnp.zeros((rows, 2 * CMP_HID), F32)
        for r in range(CMP_STRIDE):
            x = jnp.concatenate(
                [ref[0, 0, pl.ds(r, cpp, stride=CMP_STRIDE)].reshape(cpp * NSA_KV_HEADS, NSA_DH) for ref in refs],
                axis=0)
            tot = tot + _dot(x.astype(BF), wab[r])
        ab[pl.ds(pl.multiple_of(step * rows, rows), rows), :] = tot

    @pl.when(step == pl.num_programs(1) - 1)
    def _():
        n_rows = nchunks * NSA_KV_HEADS
        kv = []
        for t, ab in enumerate((abk, abv)):
            a = ab[:, :CMP_HID]
            b = pltpu.roll(ab[:, CMP_HID:], n_rows - NSA_KV_HEADS, 0)
            hid = jax.nn.silu(a + b + bias_ref[t])
            kv.append(_dot(hid.astype(BF), w2_ref[t]).astype(BF))
        kc, vc = kv
        s = _dot_nt(q_ref[0], kc) * NSA_SCALE
        col = lax.broadcasted_iota(jnp.int32, (NSA_HEADS, n_rows), 1)
        row = lax.broadcasted_iota(jnp.int32, (NSA_HEADS, n_rows), 0)
        chunk = col // NSA_KV_HEADS
        mask = ((col % NSA_KV_HEADS) == (row // NSA_GROUP)) & (chunk < nchunks - 1)
        mask = mask & (chunk * CMP_STRIDE + (CMP_BLOCK - 1) <= qpos)
        p = _masked_softmax(s, mask)
        oc_out[0] = _dot(p.astype(BF), vc)
        for g in range(NSA_KV_HEADS):
            imp_out[0, g:g + 1, :] = jnp.sum(p[g * NSA_GROUP:(g + 1) * NSA_GROUP], axis=0, keepdims=True)


def _nsa_cmp_decode(page_table, li, q16, wabk, wabv, bias, w2, cache_ck, cache_cv, qpos):
    nseq, npages = page_table.shape
    gp = math.gcd(DECODE_PAGES_PER_STEP, npages)
    page = cache_ck.shape[2]
    nchunks = npages * page // CMP_STRIDE
    n_rows = nchunks * NSA_KV_HEADS
    per_seq = lambda shp: pl.BlockSpec((1,) + shp, lambda n, s, pt: (n, 0, 0))
    full3 = lambda shp: pl.BlockSpec(shp, lambda n, s, pt: (0, 0, 0))
    in_specs = [per_seq((NSA_HEADS, NSA_DH)), full3((CMP_STRIDE, NSA_DH, 2 * CMP_HID)),
                full3((CMP_STRIDE, NSA_DH, 2 * CMP_HID)), full3((2, 1, CMP_HID)), full3((2, CMP_HID, NSA_DH))]
    in_specs += _paged_specs((page, NSA_KV_HEADS, NSA_DH), li, gp) * 2
    return pl.pallas_call(
        functools.partial(_nsa_cmp_decode_body, gp=gp, nchunks=nchunks, qpos=qpos),
        grid_spec=pltpu.PrefetchScalarGridSpec(
            num_scalar_prefetch=1,
            grid=(nseq, npages // gp),
            in_specs=in_specs,
            out_specs=[per_seq((NSA_HEADS, NSA_DH)), per_seq((NSA_KV_HEADS, n_rows))],
            scratch_shapes=[pltpu.VMEM((n_rows, 2 * CMP_HID), F32), pltpu.VMEM((n_rows, 2 * CMP_HID), F32)],
        ),
        out_shape=[SDS((nseq, NSA_HEADS, NSA_DH), F32), SDS((nseq, NSA_KV_HEADS, n_rows), F32)],
        name="nsa_cmp_decode",
    )(page_table, q16, wabk, wabv, bias, w2, *([cache_ck] * gp), *([cache_cv] * gp))


def _imp_topk_body(imp_ref, mb_ref, idx_ref, *, cur, nsb, width):
    imp_blk = _dot_x3(imp_ref[...], mb_ref[...])
    rows = imp_blk.shape[0]
    lane = lax.broadcasted_iota(jnp.int32, (rows, width), 1)
    vals = jnp.where(lane < cur, imp_blk, NEG)
    vals = jnp.where(lane < nsb, vals, LOWEST)
    _, idxs = _topk_pick(vals, min(N_SEL, nsb), width)
    out = jnp.zeros((rows, LANES), F32)
    lane_o = lax.broadcasted_iota(jnp.int32, (rows, LANES), 1)
    for t, idx in enumerate(idxs):
        out = jnp.where(lane_o == t, idx, out)
    idx_ref[...] = out.astype(jnp.int32)


def _imp_topk(imp, nchunks, nsb, cur):
    rows, n_rows = imp.shape
    width = -(-nsb // LANES) * LANES
    per = SEL_BLOCK // CMP_STRIDE
    c = (jnp.arange(n_rows) // NSA_KV_HEADS)[:, None]
    j = jnp.arange(width)[None, :]
    mb = ((c >= per * j - 1) & (c <= per * j + per - 1) & (c < nchunks - 1) & (j < nsb)).astype(BF)
    return pl.pallas_call(
        functools.partial(_imp_topk_body, cur=cur, nsb=nsb, width=width),
        grid=(1,),
        in_specs=[_full((rows, n_rows)), _full((n_rows, width))],
        out_specs=_full((rows, LANES)),
        out_shape=SDS((rows, LANES), jnp.int32),
        name="imp_topk",
    )(imp, mb)


def _nsa_sel_decode_body(pt_ref, idx_ref, val_ref, qr_ref, gt_ref, oc_ref, skn_ref, svn_ref, wkn_ref, wvn_ref,
                         wink_ref, winv_ref, *rest, nslots, wb):
    ns = nslots * NSA_KV_HEADS
    selk_refs, selv_refs = rest[0:ns], rest[ns:2 * ns]
    o_out, wink_out, winv_out = rest[2 * ns:]
    n = pl.program_id(0)
    gates = jax.nn.sigmoid(gt_ref[0])
    width = nslots * SEL_BLOCK
    blk = lax.broadcasted_iota(jnp.int32, (1, width), 1) // SEL_BLOCK
    for g in range(NSA_KV_HEADS):
        qg = qr_ref[0, g * NSA_GROUP:(g + 1) * NSA_GROUP]
        qf = qg.astype(F32)

        def attend(kcat, vcat, mask, k_new, v_new):
            s = jnp.where(mask, _dot_nt(qg, kcat) * NSA_SCALE, NEG)
            s_new = jnp.sum(qf * k_new.astype(BF).astype(F32), axis=-1, keepdims=True) * NSA_SCALE
            mx = jnp.maximum(s.max(-1, keepdims=True), s_new)
            e = jnp.where(mask, jnp.exp(s - mx), 0.0)
            e_new = jnp.exp(s_new - mx)
            den = jnp.maximum(e.sum(-1, keepdims=True) + e_new, TINY)
            num = _dot(e.astype(BF), vcat) + e_new.astype(BF).astype(F32) * v_new.astype(BF).astype(F32)
            return num / den

        kcat = jnp.concatenate([selk_refs[g * nslots + t][0, 0, :, g, :] for t in range(nslots)], axis=0).astype(BF)
        vcat = jnp.concatenate([selv_refs[g * nslots + t][0, 0, :, g, :] for t in range(nslots)], axis=0).astype(BF)
        valid = jnp.zeros((1, width), jnp.int32)
        for t in range(nslots):
            valid = jnp.where(blk == t, val_ref[(n * NSA_KV_HEADS + g) * nslots + t], valid)
        o_s = attend(kcat, vcat, valid > 0, skn_ref[0, g:g + 1], svn_ref[0, g:g + 1])
        kw = wink_ref[0, 0, :, g, :].astype(BF)
        vw = winv_ref[0, 0, :, g, :].astype(BF)
        wmask = lax.broadcasted_iota(jnp.int32, (1, wb), 1) > wb - WINDOW
        o_w = attend(kw, vw, wmask, wkn_ref[0, g:g + 1], wvn_ref[0, g:g + 1])
        sl = slice(g * NSA_GROUP, (g + 1) * NSA_GROUP)
        gg = gates[sl]
        o_out[0, sl] = gg[:, 0:1] * oc_ref[0, sl] + gg[:, 1:2] * o_s + gg[:, 2:3] * o_w
    for win_ref, new_ref, out_ref in ((wink_ref, wkn_ref, wink_out), (winv_ref, wvn_ref, winv_out)):
        out_ref[0, 0, 0:wb - 1] = win_ref[0, 0, 1:wb]
        out_ref[0, 0, wb - 1] = new_ref[0]


def _nsa_sel_decode(page_table, idx, valid, li, qr16, gt, oc, skn, svn, wkn, wvn, win_k, win_v, cache_sk, cache_sv):
    nseq = page_table.shape[0]
    page = cache_sk.shape[2]
    bpp = page // SEL_BLOCK
    nslots = idx.shape[1]
    idx, valid = idx.reshape(-1), valid.reshape(-1)
    wb = win_k.shape[2]
    per_seq = lambda shp: pl.BlockSpec((1,) + shp, lambda n, pt, ix, vl: (n, 0, 0))
    wspec = pl.BlockSpec((1, 1, wb, NSA_KV_HEADS, NSA_DH), lambda n, pt, ix, vl: (li, n, 0, 0, 0))
    sel_specs = []
    for g in range(NSA_KV_HEADS):
        for t in range(nslots):
            sel_specs.append(pl.BlockSpec(
                (1, 1, SEL_BLOCK, NSA_KV_HEADS, NSA_DH),
                lambda n, pt, ix, vl, g=g, t=t: (li, pt[n, ix[(n * NSA_KV_HEADS + g) * nslots + t] // bpp],
                                                 ix[(n * NSA_KV_HEADS + g) * nslots + t] % bpp, 0, 0)))
    in_specs = [per_seq((NSA_HEADS, NSA_DH)), per_seq((NSA_HEADS, 3)), per_seq((NSA_HEADS, NSA_DH))]
    in_specs += [per_seq((NSA_KV_HEADS, NSA_DH))] * 4 + [wspec, wspec] + sel_specs * 2
    ns = nslots * NSA_KV_HEADS
    return pl.pallas_call(
        functools.partial(_nsa_sel_decode_body, nslots=nslots, wb=wb),
        grid_spec=pltpu.PrefetchScalarGridSpec(
            num_scalar_prefetch=3,
            grid=(nseq,),
            in_specs=in_specs,
            out_specs=[per_seq((NSA_HEADS, NSA_DH)),
                       pl.BlockSpec((1, 1, wb, NSA_KV_HEADS, NSA_DH), lambda n, pt, ix, vl: (0, n, 0, 0, 0)),
                       pl.BlockSpec((1, 1, wb, NSA_KV_HEADS, NSA_DH), lambda n, pt, ix, vl: (0, n, 0, 0, 0))],
        ),
        out_shape=[SDS((nseq, NSA_HEADS, NSA_DH), F32), SDS((1, nseq, wb, NSA_KV_HEADS, NSA_DH), F32),
                   SDS((1, nseq, wb, NSA_KV_HEADS, NSA_DH), F32)],
        name="nsa_sel_decode",
    )(page_table, idx, valid, qr16, gt, oc, skn, svn, wkn, wvn, win_k, win_v, *([cache_sk] * ns), *([cache_sv] * ns))


def _cmp_weights(cmp_pos, w1, w2):
    half = CMP_STRIDE * NSA_DH
    w1ab = jnp.concatenate([w1[:, :half], w1[:, half:]], axis=2).astype(BF)
    pos = jnp.zeros((2, 8, 2 * half), F32).at[:, 0].set(cmp_pos.reshape(2, 2 * half))
    bias = _batched_matmul(pos, w1.astype(BF), F32)[:, 0:1]
    return w1ab, bias


def _cross_sublayer(x, g_pre, g_post, w_q, w_o, mk, mv, layer, n, t):
    q = _norm_matmul(x, g_pre, w_q, BF).reshape(n, t, MEM_HEADS * MEM_DH)
    o = _cross(q, mk, mv, layer).reshape(n * t, MEM_HEADS * MEM_DH)
    return _matmul_norm_res(o, w_o, x, g_post)


def kernel(x_prompt, x_sample, cache_mla_latent, cache_mla_krope, cache_diff_k, cache_diff_v, cache_nsa_cmp_k, cache_nsa_cmp_v, cache_nsa_sel_k, cache_nsa_sel_v, state_nsa_win_k, state_nsa_win_v, cache_mem_k, cache_mem_v, page_table, mem_prompt, norm_g, ffn_w_gu, ffn_w_down, mem_norm_g, mem_w_q, mem_w_k, mem_w_v, mem_w_o, even_w_in, even_w_out, mla_q_norm_g, mla_kv_norm_g, mla_w_uq, mla_w_uk, mla_w_uv, diff_lambda, diff_norm_g, odd_w_in, odd_w_out, nsa_cmp_pos, nsa_cmp_w1, nsa_cmp_w2):
    n, s, d = x_prompt.shape
    nseq, t_dec, _ = x_sample.shape
    assert t_dec == 1
    npages = page_table.shape[1]
    page = cache_mla_latent.shape[2]
    past = npages * page
    depth = norm_g.shape[0]
    mem = mem_prompt.shape[1]
    xp = x_prompt.reshape(n * s, d)
    xs = x_sample.reshape(nseq, d)
    pos_p = jnp.arange(s)
    pos_s = jnp.full((nseq,), past, jnp.int32)
    gain = lambda layer, i: norm_g[layer, i][None, :]

    even_p, even_s, odd_p, odd_s, mem_p = [], [], [], [], []
    for layer in range(depth):
        li = layer // 2
        w_gu = ffn_w_gu[layer].astype(BF)
        w_dn = ffn_w_down[layer].astype(BF)
        wkv = jnp.concatenate([mem_w_k[layer], mem_w_v[layer]], axis=1).astype(BF)
        mkv = _norm_matmul(mem_prompt.reshape(n * mem, d), mem_norm_g[layer][None, :], wkv, F32)
        mw = MEM_HEADS * MEM_DH
        mk_p = mkv[:, :mw].reshape(1, n, mem, MEM_HEADS, MEM_DH)
        mv_p = mkv[:, mw:].reshape(1, n, mem, MEM_HEADS, MEM_DH)
        mem_p.append((mk_p[0], mv_p[0]))

        xp = _ffn(xp, gain(layer, 0), gain(layer, 1), w_gu[0], w_dn[0])
        xs = _ffn(xs, gain(layer, 0), gain(layer, 1), w_gu[0], w_dn[0])

        if layer % 2 == 0:
            lam_init = 0.8 - 0.6 * math.exp(-0.3 * layer)
            ew = _even_weights(even_w_in[li], mla_w_uq[li], mla_w_uk[li], mla_w_uv[li])
            w_out = even_w_out[li].astype(BF)
            qg, kvg = mla_q_norm_g[li][None, :], mla_kv_norm_g[li][None, :]
            dg = jnp.tile(diff_norm_g[li], DIFF_HEADS)[None, :]
            (q, k, v, ckv, kr, dq, dk, dkb, dv, dvb) = _even_proj(xp, gain(layer, 2), qg, kvg, ew, _even_tables(pos_p))
            o_mla = _mla_prompt(q, k, v, n, s)
            o0, o1 = _diff_prompt(dq, dkb, dvb, n, s)
            xp = _even_finish(o_mla, o0, o1, diff_lambda[li], dg, w_out, xp, gain(layer, 3), lam_init)
            even_p.append((ckv.reshape(n, s, KV_LORA), kr[:, :MLA_ROPE].reshape(n, s, MLA_ROPE),
                           dk.reshape(n, s, DIFF_HEADS, 2 * DIFF_DH), dv.reshape(n, s, DIFF_HEADS, DIFF_DV)))
            (q, _, _, ckv, kr, dq, dk, _, dv, _) = _even_proj(xs, gain(layer, 2), qg, kvg, ew, _even_tables(pos_s))
            qabs = _batched_matmul(jnp.transpose(q.reshape(nseq, MLA_HEADS, LANES), (1, 0, 2)), ew["wabs"], BF)
            qd = jnp.pad(jnp.transpose(qabs, (1, 0, 2)), ((0, 0), (0, 16 - MLA_HEADS), (0, 0)))
            dq4 = dq.reshape(nseq, DIFF_HEADS, 1, 2, DIFF_DH)
            eye = jnp.eye(2, dtype=dq.dtype)[None, None, :, :, None]
            dq8 = (dq4 * eye).reshape(nseq, 2 * DIFF_HEADS, 2 * DIFF_DH)
            dkn = jnp.repeat(dk.reshape(nseq, DIFF_HEADS, 2 * DIFF_DH), 2, axis=1)
            dvn = jnp.repeat(dv.reshape(nseq, DIFF_HEADS, DIFF_DV), 2, axis=1)
            knew = jnp.concatenate([ckv, kr], axis=1)[:, None, :]
            lat, od = _even_decode(page_table, li, qd, dq8, knew, dkn, dvn, cache_mla_latent, cache_mla_krope,
                                   cache_diff_k, cache_diff_v)
            lat_h = jnp.transpose(lat[:, :MLA_HEADS], (1, 0, 2))
            o_mla = jnp.transpose(_batched_matmul(lat_h, ew["wuv3"], BF), (1, 0, 2)).reshape(nseq, -1)
            od = od.reshape(nseq, DIFF_HEADS, 2, DIFF_DV)
            o0 = od[:, :, 0].reshape(nseq, -1)
            o1 = od[:, :, 1].reshape(nseq, -1)
            xs = _even_finish(o_mla, o0, o1, diff_lambda[li], dg, w_out, xs, gain(layer, 3), lam_init)
            even_s.append((ckv.reshape(nseq, 1, KV_LORA), kr[:, :MLA_ROPE].reshape(nseq, 1, MLA_ROPE),
                           dk.reshape(nseq, 1, DIFF_HEADS, 2 * DIFF_DH), dv.reshape(nseq, 1, DIFF_HEADS, DIFF_DV)))
        else:
            w1 = _odd_weights(odd_w_in[li])
            w_out = odd_w_out[li].astype(BF)
            w1ab, bias = _cmp_weights(nsa_cmp_pos[li], nsa_cmp_w1[li], nsa_cmp_w2[li])
            w2 = nsa_cmp_w2[li]
            kv4 = lambda z, m_, t_: z.reshape(m_, t_, NSA_KV_HEADS, NSA_DH)
            c64, s64 = _rope_tables(pos_p, NSA_DH)
            (q, qr, ck, cv, sk, sv, wk, wv, gt, skd, svd, wkd, wvd) = _odd_proj(
                xp, gain(layer, 2), w1, jnp.tile(c64, (1, 2)), jnp.tile(s64, (1, 2)))
            nc = s // CMP_STRIDE
            chunks = lambda z: jnp.transpose(z.reshape(n, nc, CMP_STRIDE, NSA_KV_HEADS, NSA_DH), (0, 3, 1, 2, 4)
                                             ).reshape(n * NSA_KV_HEADS, nc, CMP_STRIDE * NSA_DH)
            ch = jnp.stack([chunks(ck), chunks(cv)]).astype(BF)
            cmp = _cmp_mlp(ch, w1ab, bias, jnp.concatenate([w2, w2], axis=2).astype(BF))
            o = _nsa_prompt(q, qr, gt, cmp, skd, svd, wkd, wvd, n, s)
            xp = _matmul_norm_res(o, w_out, xp, gain(layer, 3))
            win = min(WINDOW, s)
            odd_p.append((kv4(ck, n, s), kv4(cv, n, s), kv4(sk, n, s), kv4(sv, n, s),
                          kv4(wk, n, s)[:, -win:], kv4(wv, n, s)[:, -win:]))
            c64, s64 = _rope_tables(pos_s, NSA_DH)
            (q, qr, ck, cv, sk, sv, wk, wv, gt, _, _, _, _) = _odd_proj(
                xs, gain(layer, 2), w1, jnp.tile(c64, (1, 2)), jnp.tile(s64, (1, 2)))
            half = CMP_STRIDE * NSA_DH
            wab = w1ab.reshape(2, CMP_STRIDE, NSA_DH, 2 * CMP_HID)
            oc, imp = _nsa_cmp_decode(page_table, li, q.reshape(nseq, NSA_HEADS, NSA_DH), wab[0], wab[1], bias,
                                      w2.astype(BF), cache_nsa_cmp_k, cache_nsa_cmp_v, past)
            nchunks = past // CMP_STRIDE
            nsb = -(-(past + 1) // SEL_BLOCK)
            cur = past // SEL_BLOCK
            ksel = min(N_SEL, nsb)
            top = _imp_topk(imp.reshape(nseq * NSA_KV_HEADS, -1), nchunks, nsb, cur)[:, :ksel]
            valid = (top < cur).astype(jnp.int32)
            idx = jnp.where(top < cur, top, 0)
            gt3 = gt[:, :3 * NSA_HEADS].reshape(nseq, NSA_HEADS, 3)
            g2 = lambda z: z.reshape(nseq, NSA_KV_HEADS, NSA_DH)
            o, wk_new, wv_new = _nsa_sel_decode(
                page_table, idx, valid, li, qr.reshape(nseq, NSA_HEADS, NSA_DH), gt3, oc, g2(sk), g2(sv), g2(wk),
                g2(wv), state_nsa_win_k, state_nsa_win_v, cache_nsa_sel_k, cache_nsa_sel_v)
            xs = _matmul_norm_res(o.reshape(nseq, NSA_HEADS * NSA_DH), w_out, xs, gain(layer, 3))
            odd_s.append((kv4(ck, nseq, 1), kv4(cv, nseq, 1), kv4(sk, nseq, 1), kv4(sv, nseq, 1),
                          wk_new[0], wv_new[0]))

        w_q = mem_w_q[layer].astype(BF)
        w_o = mem_w_o[layer].astype(BF)
        xp = _cross_sublayer(xp, gain(layer, 4), gain(layer, 5), w_q, w_o, mk_p, mv_p, 0, n, s)
        xs = _cross_sublayer(xs, gain(layer, 4), gain(layer, 5), w_q, w_o, cache_mem_k, cache_mem_v, layer, nseq, 1)
        xp = _ffn(xp, gain(layer, 6), gain(layer, 7), w_gu[1], w_dn[1])
        xs = _ffn(xs, gain(layer, 6), gain(layer, 7), w_gu[1], w_dn[1])

    st = lambda lst, i: jnp.stack([e[i] for e in lst])
    return (xp.reshape(n, s, d), xs.reshape(nseq, 1, d),
            st(even_p, 0), st(even_p, 1), st(even_p, 2), st(even_p, 3),
            st(odd_p, 0), st(odd_p, 1), st(odd_p, 2), st(odd_p, 3), st(odd_p, 4), st(odd_p, 5),
            st(mem_p, 0), st(mem_p, 1),
            st(even_s, 0), st(even_s, 1), st(even_s, 2), st(even_s, 3),
            st(odd_s, 0), st(odd_s, 1), st(odd_s, 2), st(odd_s, 3), st(odd_s, 4), st(odd_s, 5))
```

```python
import functools
import math

import jax
import jax.numpy as jnp
from jax import lax
from jax.experimental import pallas as pl
from jax.experimental.pallas import tpu as pltpu

BF = jnp.bfloat16
F32 = jnp.float32
SDS = jax.ShapeDtypeStruct

MLA_HEADS, MLA_NOPE, MLA_ROPE, MLA_DV = 12, 64, 32, 64
Q_LORA, KV_LORA = 384, 256
DIFF_HEADS, DIFF_DH, DIFF_DV = 4, 32, 64
NSA_HEADS, NSA_KV_HEADS, NSA_DH = 16, 2, 64
NSA_GROUP = NSA_HEADS // NSA_KV_HEADS
CMP_STRIDE, CMP_BLOCK, CMP_HID = 16, 32, 128
SEL_BLOCK, N_SEL, WINDOW = 64, 15, 512
MEM_HEADS, MEM_DH = 4, 128
HALF_STEP = 0.5
ROPE_THETA = 10000.0
RMS_EPS = 1e-6
NEG = -1e30
TINY = 1e-30
LOWEST = -3e38
MLA_SCALE = (MLA_NOPE + MLA_ROPE) ** -0.5
DIFF_SCALE = DIFF_DH ** -0.5
NSA_SCALE = NSA_DH ** -0.5
MEM_SCALE = MEM_DH ** -0.5
LOG2E = math.log2(math.e)
LANES = 128
EVEN_DECODE_PAGES_PER_STEP = 16
CMP_DECODE_PAGES_PER_STEP = 32
NSA_TQ, NSA_TK_SEL, NSA_TK_WIN = 256, 1024, 512


def _dot(a, b):
    return jnp.dot(a, b, preferred_element_type=F32)


def _dot_nt(a, b):
    return lax.dot_general(a, b, (((1,), (1,)), ((), ())), preferred_element_type=F32)


def _dot_x3(x, w):
    h1 = x.astype(BF)
    r1 = x - h1.astype(F32)
    h2 = r1.astype(BF)
    h3 = (r1 - h2.astype(F32)).astype(BF)
    return _dot(h1, w) + _dot(h2, w) + _dot(h3, w)


def _rms(x, g):
    return x * lax.rsqrt(jnp.mean(x * x, axis=-1, keepdims=True) + RMS_EPS) * g


def _masked_softmax(s, mask):
    s = jnp.where(mask, s, NEG)
    e = jnp.where(mask, jnp.exp(s - s.max(-1, keepdims=True)), 0.0)
    return e / jnp.maximum(e.sum(-1, keepdims=True), TINY)


def _tile(m, pref):
    if m <= pref:
        return m
    t = pref
    while m % t:
        t //= 2
    return t


def _full(shape):
    n = len(shape)
    return pl.BlockSpec(shape, lambda *a: (0,) * n)


def _ffn_body(x_ref, gpre_ref, gpost_ref, wg_ref, wu_ref, wd_ref, o_ref, xn_ref, acc_ref, *, nf):
    j = pl.program_id(1)

    @pl.when(j == 0)
    def _():
        xn_ref[...] = _rms(x_ref[...], gpre_ref[...]).astype(BF)
        acc_ref[...] = jnp.zeros_like(acc_ref)

    xn = xn_ref[...]
    g = _dot(xn, wg_ref[...])
    u = _dot(xn, wu_ref[...])
    h = (jax.nn.silu(g) * u).astype(BF)
    acc_ref[...] += _dot(h, wd_ref[...])

    @pl.when(j == nf - 1)
    def _():
        o_ref[...] = x_ref[...] + HALF_STEP * _rms(acc_ref[...], gpost_ref[...])


def _ffn(x, gpre, gpost, w_gu, w_down):
    m, d = x.shape
    f = w_down.shape[0]
    tf = 256
    nf = f // tf
    tm = _tile(m, 1024)
    return pl.pallas_call(
        functools.partial(_ffn_body, nf=nf),
        grid=(m // tm, nf),
        in_specs=[
            pl.BlockSpec((tm, d), lambda i, j: (i, 0)),
            pl.BlockSpec((1, d), lambda i, j: (0, 0)),
            pl.BlockSpec((1, d), lambda i, j: (0, 0)),
            pl.BlockSpec((d, tf), lambda i, j: (0, j)),
            pl.BlockSpec((d, tf), lambda i, j: (0, j + nf)),
            pl.BlockSpec((tf, d), lambda i, j: (j, 0)),
        ],
        out_specs=pl.BlockSpec((tm, d), lambda i, j: (i, 0)),
        out_shape=SDS((m, d), F32),
        scratch_shapes=[pltpu.VMEM((tm, d), BF), pltpu.VMEM((tm, d), F32)],
        name="ffn",
    )(x, gpre, gpost, w_gu, w_gu, w_down)


def _norm_matmul_body(x_ref, g_ref, w_ref, o_ref, xn_ref):
    @pl.when(pl.program_id(1) == 0)
    def _():
        xn_ref[...] = _rms(x_ref[...], g_ref[...]).astype(BF)

    o_ref[...] = _dot(xn_ref[...], w_ref[...]).astype(o_ref.dtype)


def _norm_matmul(x, g, w, out_dtype):
    m, d = x.shape
    n = w.shape[1]
    tm = _tile(m, 512)
    tn = _tile(n, 512)
    return pl.pallas_call(
        _norm_matmul_body,
        grid=(m // tm, n // tn),
        in_specs=[
            pl.BlockSpec((tm, d), lambda i, j: (i, 0)),
            pl.BlockSpec((1, d), lambda i, j: (0, 0)),
            pl.BlockSpec((d, tn), lambda i, j: (0, j)),
        ],
        out_specs=pl.BlockSpec((tm, tn), lambda i, j: (i, j)),
        out_shape=SDS((m, n), out_dtype),
        scratch_shapes=[pltpu.VMEM((tm, d), BF)],
        name="norm_matmul",
    )(x, g, w)


def _matmul_norm_res_body(a_ref, w_ref, x_ref, g_ref, o_ref):
    y = _dot(a_ref[...].astype(BF), w_ref[...])
    o_ref[...] = x_ref[...] + _rms(y, g_ref[...])


def _matmul_norm_res(a, w, x, g):
    m, k = a.shape
    d = w.shape[1]
    tm = _tile(m, 512)
    return pl.pallas_call(
        _matmul_norm_res_body,
        grid=(m // tm,),
        in_specs=[
            pl.BlockSpec((tm, k), lambda i: (i, 0)),
            pl.BlockSpec((k, d), lambda i: (0, 0)),
            pl.BlockSpec((tm, d), lambda i: (i, 0)),
            pl.BlockSpec((1, d), lambda i: (0, 0)),
        ],
        out_specs=pl.BlockSpec((tm, d), lambda i: (i, 0)),
        out_shape=SDS((m, d), F32),
        name="matmul_norm_res",
    )(a, w, x, g)


def _batched_matmul_body(a_ref, w_ref, o_ref):
    o_ref[0] = _dot(a_ref[0].astype(BF), w_ref[0]).astype(o_ref.dtype)


def _batched_matmul(a, w, out_dtype):
    b, m, k = a.shape
    n = w.shape[2]
    return pl.pallas_call(
        _batched_matmul_body,
        grid=(b,),
        in_specs=[pl.BlockSpec((1, m, k), lambda i: (i, 0, 0)), pl.BlockSpec((1, k, n), lambda i: (i, 0, 0))],
        out_specs=pl.BlockSpec((1, m, n), lambda i: (i, 0, 0)),
        out_shape=SDS((b, m, n), out_dtype),
        name="batched_matmul",
    )(a, w)


def _rope_tables(pos, d):
    half = d // 2
    inv = ROPE_THETA ** (-jnp.arange(half, dtype=F32) * (2.0 / d))
    ang = pos.astype(F32)[:, None] * inv[None, :]
    c, s = jnp.cos(ang), jnp.sin(ang)
    return jnp.concatenate([c, c], -1), jnp.concatenate([s, s], -1)


def _rot_cols(w, d):
    k, n = w.shape
    wr = w.reshape(k, n // d, 2, d // 2)
    return jnp.concatenate([-wr[:, :, 1], wr[:, :, 0]], axis=2).reshape(k, n)


def _pad_cols(w, n):
    return jnp.pad(w, ((0, 0), (0, n - w.shape[1])))


E_CQ, E_CKV, E_DV, E_DQ, E_DK, E_KR = 0, 384, 640, 896, 1152, 1408
E_DQR, E_DKR, E_KRR, E_W = 1536, 1792, 2048, 2176
QW = MLA_HEADS * LANES


def _even_weights(w_in, w_uq, w_uk, w_uv):
    dqk = DIFF_HEADS * 2 * DIFF_DH
    o = [0, Q_LORA, Q_LORA + KV_LORA, Q_LORA + KV_LORA + MLA_ROPE]
    c_q, c_kv, k_r = w_in[:, o[0]:o[1]], w_in[:, o[1]:o[2]], w_in[:, o[2]:o[3]]
    dq = w_in[:, o[3]:o[3] + dqk]
    dk = w_in[:, o[3] + dqk:o[3] + 2 * dqk]
    dv = w_in[:, o[3] + 2 * dqk:]
    w1 = jnp.concatenate([c_q, c_kv, dv, dq, dk, _pad_cols(k_r, LANES), _rot_cols(dq, DIFF_DH),
                          _rot_cols(dk, DIFF_DH), _pad_cols(_rot_cols(k_r, MLA_ROPE), LANES)], axis=1).astype(BF)
    uq = w_uq.reshape(Q_LORA, MLA_HEADS, MLA_NOPE + MLA_ROPE)
    z = jnp.zeros((Q_LORA, MLA_HEADS, LANES - MLA_NOPE - MLA_ROPE), w_uq.dtype)
    wq = jnp.concatenate([uq, z], axis=2).reshape(Q_LORA, QW).astype(BF)
    rot = _rot_cols(uq[:, :, MLA_NOPE:].reshape(Q_LORA, -1), MLA_ROPE).reshape(Q_LORA, MLA_HEADS, MLA_ROPE)
    wqr = jnp.concatenate([jnp.zeros_like(uq[:, :, :MLA_NOPE]), rot, z], axis=2).reshape(Q_LORA, QW).astype(BF)
    uk = w_uk.reshape(KV_LORA, MLA_HEADS, MLA_NOPE)
    wk_lat = jnp.concatenate([uk, jnp.zeros((KV_LORA, MLA_HEADS, LANES - MLA_NOPE), w_uk.dtype)], axis=2)
    wk_lat = wk_lat.reshape(KV_LORA, QW).astype(BF)
    place = jnp.zeros((LANES, LANES), F32).at[jnp.arange(MLA_ROPE), MLA_NOPE + jnp.arange(MLA_ROPE)].set(1.0)
    wk_rope = jnp.tile(place, (1, MLA_HEADS)).astype(BF)
    wabs = jnp.zeros((MLA_HEADS, LANES, KV_LORA + LANES), F32)
    wabs = wabs.at[:, :MLA_NOPE, :KV_LORA].set(jnp.transpose(uk, (1, 2, 0)))
    wabs = wabs.at[:, MLA_NOPE + jnp.arange(MLA_ROPE), KV_LORA + jnp.arange(MLA_ROPE)].set(1.0)
    wuv3 = jnp.transpose(w_uv.reshape(KV_LORA, MLA_HEADS, MLA_DV), (1, 0, 2)).astype(BF)
    return dict(w1=w1, wq=wq, wqr=wqr, wk_lat=wk_lat, wk_rope=wk_rope, wuv=w_uv.astype(BF),
                wabs=wabs.astype(BF), wuv3=wuv3)


def _even_tables(pos):
    c32, s32 = _rope_tables(pos, MLA_ROPE)
    t = pos.shape[0]
    cos256 = jnp.tile(c32, (1, 8))
    sin256 = jnp.tile(s32, (1, 8))
    cosq = jnp.concatenate([jnp.ones((t, MLA_NOPE), F32), c32, jnp.zeros((t, 32), F32)], axis=1)
    sinq = jnp.concatenate([jnp.zeros((t, MLA_NOPE), F32), s32, jnp.zeros((t, 32), F32)], axis=1)
    return cos256, sin256, cosq, sinq


def _even_proj_body(x_ref, g_ref, qg_ref, kvg_ref, w1_ref, wq_ref, wqr_ref, wkl_ref, wkr_ref, wuv_ref,
                    c256_ref, s256_ref, cq_ref, sq_ref,
                    q_out, k_out, v_out, ckv_out, kr_out, dq_out, dk_out, dkb_out, dv_out, dvb_out):
    xn = _rms(x_ref[...], g_ref[...]).astype(BF)
    proj = _dot(xn, w1_ref[...])
    c256, s256 = c256_ref[...], s256_ref[...]
    dq = proj[:, E_DQ:E_DQ + 256] * c256 + proj[:, E_DQR:E_DQR + 256] * s256
    dk = proj[:, E_DK:E_DK + 256] * c256 + proj[:, E_DKR:E_DKR + 256] * s256
    kr = proj[:, E_KR:E_KR + LANES] * c256[:, :LANES] + proj[:, E_KRR:E_KRR + LANES] * s256[:, :LANES]
    dv = proj[:, E_DV:E_DV + 256]
    cqn = _rms(proj[:, E_CQ:E_CQ + Q_LORA], qg_ref[...]).astype(BF)
    ckv = _rms(proj[:, E_CKV:E_CKV + KV_LORA], kvg_ref[...])
    ckv_b = ckv.astype(BF)
    q = _dot(cqn, wq_ref[...])
    qr = _dot(cqn, wqr_ref[...])
    cq, sq = cq_ref[...], sq_ref[...]
    for h in range(MLA_HEADS):
        sl = slice(h * LANES, (h + 1) * LANES)
        q_out[:, sl] = (q[:, sl] * cq + qr[:, sl] * sq).astype(BF)
    k_out[...] = (_dot(ckv_b, wkl_ref[...]) + _dot(kr.astype(BF), wkr_ref[...])).astype(BF)
    v_out[...] = _dot(ckv_b, wuv_ref[...]).astype(BF)
    ckv_out[...] = ckv
    kr_out[...] = kr
    dq_out[...] = dq.astype(BF)
    dk_out[...] = dk
    dkb_out[...] = dk.astype(BF)
    dv_out[...] = dv
    dvb_out[...] = dv.astype(BF)


def _even_proj(x, g, q_g, kv_g, ew, tables):
    m, d = x.shape
    tm = _tile(m, 256)
    nt = tables[0].shape[0] // tm
    row = lambda w: pl.BlockSpec((tm, w), lambda i: (i, 0))
    tab = lambda w: pl.BlockSpec((tm, w), lambda i: (i % nt, 0))
    vw = MLA_HEADS * MLA_DV
    return pl.pallas_call(
        _even_proj_body,
        grid=(m // tm,),
        in_specs=[row(d), _full((1, d)), _full((1, Q_LORA)), _full((1, KV_LORA)), _full((d, E_W)),
                  _full((Q_LORA, QW)), _full((Q_LORA, QW)), _full((KV_LORA, QW)), _full((LANES, QW)),
                  _full((KV_LORA, vw)), tab(256), tab(256), tab(LANES), tab(LANES)],
        out_specs=[row(QW), row(QW), row(vw), row(KV_LORA), row(LANES), row(256), row(256), row(256), row(256),
                   row(256)],
        out_shape=[SDS((m, QW), BF), SDS((m, QW), BF), SDS((m, vw), BF), SDS((m, KV_LORA), F32),
                   SDS((m, LANES), F32), SDS((m, 256), BF), SDS((m, 256), F32), SDS((m, 256), BF),
                   SDS((m, 256), F32), SDS((m, 256), BF)],
        name="even_proj",
    )(x, g, q_g, kv_g, ew["w1"], ew["wq"], ew["wqr"], ew["wk_lat"], ew["wk_rope"], ew["wuv"], *tables)


def _mla_prompt_body(q_ref, k_ref, v_ref, o_ref, m_ref, l_ref, acc_ref, *, tq):
    i = pl.program_id(2)
    j = pl.program_id(3)
    lo_half = lax.broadcasted_iota(jnp.int32, (tq, LANES), 1) < MLA_DV

    @pl.when(j == 0)
    def _():
        m_ref[...] = jnp.full_like(m_ref, NEG)
        l_ref[...] = jnp.zeros_like(l_ref)
        acc_ref[...] = jnp.zeros_like(acc_ref)

    def step(diag):
        v = v_ref[...]
        if diag:
            mask = (lax.broadcasted_iota(jnp.int32, (tq, tq), 1) <= lax.broadcasted_iota(jnp.int32, (tq, tq), 0))
        for a in range(2):
            sl = slice(a * LANES, (a + 1) * LANES)
            s = _dot_nt(q_ref[:, sl], k_ref[:, sl]) * (MLA_SCALE * LOG2E)
            if diag:
                s = jnp.where(mask, s, NEG)
            m_old = m_ref[a]
            m_new = jnp.maximum(m_old, s.max(-1, keepdims=True))
            corr = jnp.exp2(m_old - m_new)
            p = jnp.exp2(s - m_new)
            l_ref[a] = l_ref[a] * corr + p.sum(-1, keepdims=True)
            m_ref[a] = m_new
            acc_ref[a] = acc_ref[a] * corr + _dot(p.astype(BF), v)

    @pl.when(j < i)
    def _():
        step(False)

    @pl.when(j == i)
    def _():
        step(True)
        o_ref[...] = jnp.where(lo_half, acc_ref[0] / l_ref[0], acc_ref[1] / l_ref[1]).astype(o_ref.dtype)


def _mla_prompt(q, k, v, n, s):
    tq = _tile(s, 1024)
    nq = s // tq
    pairs = MLA_HEADS // 2
    return pl.pallas_call(
        functools.partial(_mla_prompt_body, tq=tq),
        grid=(n, pairs, nq, nq),
        in_specs=[
            pl.BlockSpec((tq, 2 * LANES), lambda b, p, i, j: (b * nq + i, p)),
            pl.BlockSpec((tq, 2 * LANES), lambda b, p, i, j: (b * nq + jnp.minimum(j, i), p)),
            pl.BlockSpec((tq, LANES), lambda b, p, i, j: (b * nq + jnp.minimum(j, i), p)),
        ],
        out_specs=pl.BlockSpec((tq, LANES), lambda b, p, i, j: (b * nq + i, p)),
        out_shape=SDS((n * s, MLA_HEADS * MLA_DV), BF),
        scratch_shapes=[pltpu.VMEM((2, tq, 1), F32), pltpu.VMEM((2, tq, 1), F32), pltpu.VMEM((2, tq, LANES), F32)],
        name="mla_prompt",
    )(q, k, v)


def _diff_prompt_body(q_ref, k_ref, v_ref, o0_ref, o1_ref, m_ref, l_ref, acc_ref, *, tq):
    i = pl.program_id(1)
    j = pl.program_id(2)
    lane = lax.broadcasted_iota(jnp.int32, (tq, 256), 1)
    nmaps = 2 * DIFF_HEADS

    @pl.when(j == 0)
    def _():
        m_ref[...] = jnp.full_like(m_ref, NEG)
        l_ref[...] = jnp.zeros_like(l_ref)
        acc_ref[...] = jnp.zeros_like(acc_ref)

    def step(diag):
        q, k, v = q_ref[...], k_ref[...], v_ref[...]
        if diag:
            mask = (lax.broadcasted_iota(jnp.int32, (tq, tq), 1) <= lax.broadcasted_iota(jnp.int32, (tq, tq), 0))
        for idx in range(nmaps):
            h, mm = idx // 2, idx % 2
            qm = jnp.where((lane >= idx * DIFF_DH) & (lane < (idx + 1) * DIFF_DH), q, jnp.zeros_like(q))
            s = _dot_nt(qm, k) * DIFF_SCALE
            if diag:
                s = jnp.where(mask, s, NEG)
            m_old = m_ref[idx]
            m_new = jnp.maximum(m_old, s.max(-1, keepdims=True))
            corr = jnp.exp(m_old - m_new)
            p = jnp.exp(s - m_new)
            l_ref[idx] = l_ref[idx] * corr + p.sum(-1, keepdims=True)
            m_ref[idx] = m_new
            pv = _dot(p.astype(BF), v)
            hmask = (lane >= h * DIFF_DV) & (lane < (h + 1) * DIFF_DV)
            acc_ref[mm] = jnp.where(hmask, acc_ref[mm] * corr + pv, acc_ref[mm])

    @pl.when(j < i)
    def _():
        step(False)

    @pl.when(j == i)
    def _():
        step(True)
        for mm, o_ref in ((0, o0_ref), (1, o1_ref)):
            lfull = jnp.zeros((tq, 256), F32)
            for h in range(DIFF_HEADS):
                hmask = (lane >= h * DIFF_DV) & (lane < (h + 1) * DIFF_DV)
                lfull = jnp.where(hmask, l_ref[2 * h + mm], lfull)
            o_ref[...] = acc_ref[mm] / lfull


def _diff_prompt(dq, dk, dv, n, s):
    tq = _tile(s, 512)
    nq = s // tq
    nmaps = 2 * DIFF_HEADS
    blk = lambda f: pl.BlockSpec((tq, 256), f)
    return pl.pallas_call(
        functools.partial(_diff_prompt_body, tq=tq),
        grid=(n, nq, nq),
        in_specs=[blk(lambda b, i, j: (b * nq + i, 0)), blk(lambda b, i, j: (b * nq + jnp.minimum(j, i), 0)),
                  blk(lambda b, i, j: (b * nq + jnp.minimum(j, i), 0))],
        out_specs=[blk(lambda b, i, j: (b * nq + i, 0)), blk(lambda b, i, j: (b * nq + i, 0))],
        out_shape=[SDS((n * s, 256), F32), SDS((n * s, 256), F32)],
        scratch_shapes=[pltpu.VMEM((nmaps, tq, 1), F32), pltpu.VMEM((nmaps, tq, 1), F32),
                        pltpu.VMEM((2, tq, 256), F32)],
        name="diff_prompt",
    )(dq, dk, dv)


def _even_finish_body(om_ref, o0_ref, o1_ref, lam_ref, dg_ref, wm_ref, wd_ref, x_ref, g_ref, o_ref, *, lam_init):
    lp = lam_ref[...]
    lam = (jnp.exp(jnp.sum(lp[0:1] * lp[1:2], axis=-1, keepdims=True))
           - jnp.exp(jnp.sum(lp[2:3] * lp[3:4], axis=-1, keepdims=True)) + lam_init)
    d = o0_ref[...] - lam * o1_ref[...]
    tm = d.shape[0]
    lane = lax.broadcasted_iota(jnp.int32, (tm, 256), 1)
    d2 = d * d
    inv = jnp.zeros_like(d)
    for h in range(DIFF_HEADS):
        hmask = (lane >= h * DIFF_DV) & (lane < (h + 1) * DIFF_DV)
        ms = jnp.sum(jnp.where(hmask, d2, 0.0), axis=-1, keepdims=True) * (1.0 / DIFF_DV)
        inv = jnp.where(hmask, lax.rsqrt(ms + RMS_EPS), inv)
    dn = (d * inv * dg_ref[...]) * (1.0 - lam_init)
    y = _dot(om_ref[...], wm_ref[...]) + _dot(dn.astype(BF), wd_ref[...])
    o_ref[...] = x_ref[...] + _rms(y, g_ref[...])


def _even_finish(o_mla, o0, o1, lam_p, diff_g256, w_out, x, g, lam_init):
    m, d = x.shape
    tm = _tile(m, 512)
    vw = MLA_HEADS * MLA_DV
    row = lambda w: pl.BlockSpec((tm, w), lambda i: (i, 0))
    return pl.pallas_call(
        functools.partial(_even_finish_body, lam_init=lam_init),
        grid=(m // tm,),
        in_specs=[row(vw), row(256), row(256), _full((4, DIFF_DH)), _full((1, 256)), _full((vw, d)),
                  _full((256, d)), row(d), _full((1, d))],
        out_specs=row(d),
        out_shape=SDS((m, d), F32),
        name="even_finish",
    )(o_mla, o0, o1, lam_p, diff_g256, w_out[:vw], w_out[vw:], x, g)


def _cross_body(q_ref, mk_ref, mv_ref, o_ref):
    for h in range(MEM_HEADS):
        sl = slice(h * MEM_DH, (h + 1) * MEM_DH)
        k = mk_ref[0, 0, :, h, :].astype(BF)
        v = mv_ref[0, 0, :, h, :].astype(BF)
        s = _dot_nt(q_ref[0, :, sl], k) * MEM_SCALE
        p = jax.nn.softmax(s, axis=-1)
        o_ref[0, :, sl] = _dot(p.astype(BF), v).astype(o_ref.dtype)


def _cross(q, mk, mv, layer):
    n, t, w = q.shape
    mem = mk.shape[2]
    tq = _tile(t, 512)
    mspec = pl.BlockSpec((1, 1, mem, MEM_HEADS, MEM_DH), lambda b, i: (layer, b, 0, 0, 0))
    return pl.pallas_call(
        _cross_body,
        grid=(n, t // tq),
        in_specs=[pl.BlockSpec((1, tq, w), lambda b, i: (b, i, 0)), mspec, mspec],
        out_specs=pl.BlockSpec((1, tq, w), lambda b, i: (b, i, 0)),
        out_shape=SDS((n, t, w), BF),
        name="cross",
    )(q, mk, mv)


def _even_decode_body(pt_ref, qd_ref, dq_ref, knew_ref, dkn_ref, dvn_ref, *rest, gp):
    lat_refs, kr_refs = rest[0:gp], rest[gp:2 * gp]
    dk_refs, dv_refs = rest[2 * gp:3 * gp], rest[3 * gp:4 * gp]
    lat_out, od_out, m1, l1, acc1, m2, l2, acc2 = rest[4 * gp:]
    step = pl.program_id(1)
    qd = qd_ref[0]
    dq = dq_ref[0]

    @pl.when(step == 0)
    def _():
        kn = knew_ref[0].astype(BF)
        s1 = jnp.sum(qd.astype(F32) * kn.astype(F32), axis=-1, keepdims=True) * MLA_SCALE
        m1[...] = s1
        l1[...] = jnp.ones_like(l1)
        acc1[...] = jnp.broadcast_to(kn[:, :KV_LORA].astype(F32), acc1.shape)
        s2 = jnp.sum(dq.astype(F32) * dkn_ref[0].astype(BF).astype(F32), axis=-1, keepdims=True) * DIFF_SCALE
        m2[...] = s2
        l2[...] = jnp.ones_like(l2)
        acc2[...] = jnp.broadcast_to(dvn_ref[0].astype(BF).astype(F32), acc2.shape)

    c = jnp.concatenate([r[0, 0] for r in lat_refs], axis=0).astype(BF)
    krt = jnp.concatenate([r[0, 0] for r in kr_refs], axis=1).astype(BF)
    s = (_dot_nt(qd[:, :KV_LORA], c) + _dot(qd[:, KV_LORA:KV_LORA + MLA_ROPE], krt)) * MLA_SCALE
    m_new = jnp.maximum(m1[...], s.max(-1, keepdims=True))
    corr = jnp.exp(m1[...] - m_new)
    p = jnp.exp(s - m_new)
    l1[...] = l1[...] * corr + p.sum(-1, keepdims=True)
    acc1[...] = acc1[...] * corr + _dot(p.astype(BF), c)
    m1[...] = m_new

    kt = jnp.concatenate([r[0, 0] for r in dk_refs], axis=1).astype(BF)
    vt = jnp.concatenate([r[0, 0] for r in dv_refs], axis=1).astype(BF)
    s = _dot(dq, kt) * DIFF_SCALE
    m_new = jnp.maximum(m2[...], s.max(-1, keepdims=True))
    corr = jnp.exp(m2[...] - m_new)
    p = jnp.exp(s - m_new)
    l2[...] = l2[...] * corr + p.sum(-1, keepdims=True)
    acc2[...] = acc2[...] * corr + _dot_nt(p.astype(BF), vt)
    m2[...] = m_new

    @pl.when(step == pl.num_programs(1) - 1)
    def _():
        lat_out[0] = acc1[...] / l1[...]
        od_out[0] = acc2[...] / l2[...]


def _paged_specs(shape_tail, li, gp):
    nd = len(shape_tail)
    specs = []
    for t in range(gp):
        specs.append(pl.BlockSpec((1, 1) + shape_tail,
                                  lambda n, s, pt, t=t: (li, pt[n, s * gp + t]) + (0,) * nd))
    return specs


def _feature_major(cache):
    l, pool, rows = cache.shape[:3]
    nd = cache.ndim
    return jnp.transpose(cache, (0, 1) + tuple(range(3, nd)) + (2,)).reshape(l, pool, -1, rows)


def _even_decode(page_table, li, qd, dqb, knew, dkn, dvn, cache_lat, cache_kr, cache_dk, cache_dv):
    nseq, npages = page_table.shape
    gp = math.gcd(EVEN_DECODE_PAGES_PER_STEP, npages)
    page = cache_lat.shape[2]
    hp = qd.shape[1]
    nmaps = 2 * DIFF_HEADS
    dw = DIFF_HEADS * DIFF_DV
    krt, dkt, dvt = _feature_major(cache_kr), _feature_major(cache_dk), _feature_major(cache_dv)
    per_seq = lambda shp: pl.BlockSpec((1,) + shp, lambda n, s, pt: (n, 0, 0))
    in_specs = [per_seq((hp, KV_LORA + LANES)), per_seq((nmaps, dw)), per_seq((1, KV_LORA + LANES)),
                per_seq((1, dw)), per_seq((1, dw))]
    in_specs += _paged_specs((page, KV_LORA), li, gp) + _paged_specs((MLA_ROPE, page), li, gp)
    in_specs += _paged_specs((dw, page), li, gp) * 2
    return pl.pallas_call(
        functools.partial(_even_decode_body, gp=gp),
        grid_spec=pltpu.PrefetchScalarGridSpec(
            num_scalar_prefetch=1,
            grid=(nseq, npages // gp),
            in_specs=in_specs,
            out_specs=[per_seq((hp, KV_LORA)), per_seq((nmaps, dw))],
            scratch_shapes=[pltpu.VMEM((hp, 1), F32), pltpu.VMEM((hp, 1), F32), pltpu.VMEM((hp, KV_LORA), F32),
                            pltpu.VMEM((nmaps, 1), F32), pltpu.VMEM((nmaps, 1), F32),
                            pltpu.VMEM((nmaps, dw), F32)],
        ),
        out_shape=[SDS((nseq, hp, KV_LORA), F32), SDS((nseq, nmaps, dw), F32)],
        name="even_decode",
    )(page_table, qd, dqb, knew, dkn, dvn, *([cache_lat] * gp), *([krt] * gp), *([dkt] * gp), *([dvt] * gp))


O_Q, O_QR, O_CK, O_CV, O_SK, O_SKR, O_SV, O_WK, O_WKR, O_WV, O_GT = (
    0, 1024, 2048, 2176, 2304, 2432, 2560, 2688, 2816, 2944, 3072)
O_SKD, O_SKDR, O_SVD, O_WKD, O_WKDR, O_WVD, O_W = 3200, 3456, 3712, 3968, 4224, 4480, 4736
KVW = NSA_KV_HEADS * NSA_DH


def _dup_cols(w):
    k = w.shape[0]
    wr = w.reshape(k, NSA_KV_HEADS, 1, NSA_DH)
    return jnp.broadcast_to(wr, (k, NSA_KV_HEADS, 2, NSA_DH)).reshape(k, 2 * KVW)


def _odd_weights(w_in):
    hw = NSA_HEADS * NSA_DH
    q = w_in[:, :hw]
    ck, cv, sk, sv, wk, wv = [w_in[:, hw + t * KVW: hw + (t + 1) * KVW] for t in range(6)]
    gl = w_in[:, hw + 6 * KVW:]
    skr, wkr = _rot_cols(sk, NSA_DH), _rot_cols(wk, NSA_DH)
    return jnp.concatenate([q, _rot_cols(q, NSA_DH), ck, cv, sk, skr, sv, wk, wkr, wv, _pad_cols(gl, LANES),
                            _dup_cols(sk), _dup_cols(skr), _dup_cols(sv), _dup_cols(wk), _dup_cols(wkr),
                            _dup_cols(wv)], axis=1).astype(BF)


def _odd_proj_body(x_ref, g_ref, w_ref, cos_ref, sin_ref,
                   q_out, qr_out, ck_out, cv_out, sk_out, sv_out, wk_out, wv_out, gt_out,
                   skd_out, svd_out, wkd_out, wvd_out):
    xn = _rms(x_ref[...], g_ref[...]).astype(BF)
    proj = _dot(xn, w_ref[...])
    cos, sin = cos_ref[...], sin_ref[...]
    for c in range(NSA_HEADS * NSA_DH // LANES):
        sl = slice(c * LANES, (c + 1) * LANES)
        q = proj[:, O_Q + c * LANES:O_Q + (c + 1) * LANES]
        q_out[:, sl] = q.astype(BF)
        qr_out[:, sl] = (q * cos + proj[:, O_QR + c * LANES:O_QR + (c + 1) * LANES] * sin).astype(BF)
    ck_out[...] = proj[:, O_CK:O_CK + KVW]
    cv_out[...] = proj[:, O_CV:O_CV + KVW]
    sk_out[...] = proj[:, O_SK:O_SK + KVW] * cos + proj[:, O_SKR:O_SKR + KVW] * sin
    sv_out[...] = proj[:, O_SV:O_SV + KVW]
    wk_out[...] = proj[:, O_WK:O_WK + KVW] * cos + proj[:, O_WKR:O_WKR + KVW] * sin
    wv_out[...] = proj[:, O_WV:O_WV + KVW]
    gt_out[...] = proj[:, O_GT:O_GT + LANES]
    for g in range(NSA_KV_HEADS):
        sl = slice(g * LANES, (g + 1) * LANES)
        skd_out[:, sl] = (proj[:, O_SKD + g * LANES:O_SKD + (g + 1) * LANES] * cos
                          + proj[:, O_SKDR + g * LANES:O_SKDR + (g + 1) * LANES] * sin).astype(BF)
        wkd_out[:, sl] = (proj[:, O_WKD + g * LANES:O_WKD + (g + 1) * LANES] * cos
                          + proj[:, O_WKDR + g * LANES:O_WKDR + (g + 1) * LANES] * sin).astype(BF)
    svd_out[...] = proj[:, O_SVD:O_SVD + 2 * KVW].astype(BF)
    wvd_out[...] = proj[:, O_WVD:O_WVD + 2 * KVW].astype(BF)


def _odd_proj(x, g, w1, cos, sin):
    m, d = x.shape
    tm = _tile(m, 256)
    nt = cos.shape[0] // tm
    hw = NSA_HEADS * NSA_DH
    row = lambda w: pl.BlockSpec((tm, w), lambda i: (i, 0))
    tab = pl.BlockSpec((tm, LANES), lambda i: (i % nt, 0))
    return pl.pallas_call(
        _odd_proj_body,
        grid=(m // tm,),
        in_specs=[row(d), _full((1, d)), _full((d, O_W)), tab, tab],
        out_specs=[row(hw), row(hw)] + [row(KVW)] * 6 + [row(LANES)] + [row(2 * KVW)] * 4,
        out_shape=[SDS((m, hw), BF), SDS((m, hw), BF)] + [SDS((m, KVW), F32)] * 6 + [SDS((m, LANES), F32)]
        + [SDS((m, 2 * KVW), BF)] * 4,
        name="odd_proj",
    )(x, g, w1, cos, sin)


def _cmp_mlp_body(ch_ref, w1_ref, bias_ref, w2_ref, o_ref):
    ab = _dot(ch_ref[0, 0], w1_ref[0])
    nc = ab.shape[0]
    a = ab[:, :CMP_HID]
    b = pltpu.roll(ab[:, CMP_HID:], nc - 1, 0)
    hid = jax.nn.silu(a + b + bias_ref[0])
    o_ref[0, 0] = _dot(hid.astype(BF), w2_ref[0]).astype(o_ref.dtype)


def _cmp_mlp(ch, w1ab, bias, w2dup):
    _, b, nc, kk = ch.shape
    return pl.pallas_call(
        _cmp_mlp_body,
        grid=(2, b),
        in_specs=[pl.BlockSpec((1, 1, nc, kk), lambda t, i: (t, i, 0, 0)),
                  pl.BlockSpec((1, kk, 2 * CMP_HID), lambda t, i: (t, 0, 0)),
                  pl.BlockSpec((1, 1, CMP_HID), lambda t, i: (t, 0, 0)),
                  pl.BlockSpec((1, CMP_HID, LANES), lambda t, i: (t, 0, 0))],
        out_specs=pl.BlockSpec((1, 1, nc, LANES), lambda t, i: (t, i, 0, 0)),
        out_shape=SDS((2, b, nc, LANES), BF),
        name="cmp_mlp",
    )(ch, w1ab, bias, w2dup)


def _topk_pick(vals, k, width):
    rows = vals.shape[0]
    lane_f = lax.broadcasted_iota(jnp.int32, (rows, width), 1).astype(F32)
    picked = jnp.zeros((rows, width), jnp.bool_)
    idxs = []
    for _ in range(k):
        mx = vals.max(-1, keepdims=True)
        idx = jnp.min(jnp.where(vals == mx, lane_f, float(width)), axis=-1, keepdims=True)
        hit = lane_f == idx
        picked = jnp.logical_or(picked, hit)
        vals = jnp.where(hit, LOWEST, vals)
        idxs.append(idx)
    return picked, idxs


def _nsa_prompt_body(q_ref, qr_ref, gt_ref, kc_ref, vc_ref, ks_ref, vs_ref, kw_ref, vw_ref, e_ref, mb_ref, o_ref,
                     res_ref, m_ref, l_ref, acc_ref, *, tq, tks, tkw, nc, nsb):
    i = pl.program_id(1)
    q0 = i * tq
    qpos = q0 + lax.broadcasted_iota(jnp.int32, (tq, 1), 0)
    lane = lax.broadcasted_iota(jnp.int32, (tq, LANES), 1)
    lo_half = lane < NSA_DH
    gates = jax.nn.sigmoid(gt_ref[...])
    c_end = lax.broadcasted_iota(jnp.int32, (tq, nc), 1) * CMP_STRIDE + (CMP_BLOCK - 1)
    mask_c = c_end <= qpos
    zero_b = jnp.zeros((tq, LANES), BF)
    R = NSA_GROUP

    def stack_q(ref, g):
        parts = []
        for r in range(R):
            hd = g * R + r
            qh = ref[:, (hd // 2) * LANES:(hd // 2 + 1) * LANES]
            parts.append(jnp.where(lo_half if hd % 2 == 0 else jnp.logical_not(lo_half), qh, zero_b))
        return jnp.concatenate(parts, axis=0)

    def gate3(g, b):
        return jnp.stack([gates[:, 3 * (g * R + r) + b:3 * (g * R + r) + b + 1] for r in range(R)], axis=0)

    def online(qst, kt, vt, mask):
        for r in range(R):
            s = _dot_nt(qst[r * tq:(r + 1) * tq], kt) * (NSA_SCALE * LOG2E)
            s = jnp.where(mask, s, NEG)
            m_old = m_ref[r]
            m_new = jnp.maximum(m_old, s.max(-1, keepdims=True))
            corr = jnp.exp2(m_old - m_new)
            p = jnp.exp2(s - m_new)
            l_ref[r] = l_ref[r] * corr + p.sum(-1, keepdims=True)
            acc_ref[r] = acc_ref[r] * corr + _dot(p.astype(BF), vt)
            m_ref[r] = m_new

    def reset():
        m_ref[...] = jnp.full_like(m_ref, NEG)
        l_ref[...] = jnp.zeros_like(l_ref)
        acc_ref[...] = jnp.zeros_like(acc_ref)

    for g in range(NSA_KV_HEADS):
        gsl = slice(g * LANES, (g + 1) * LANES)
        hsl = slice(g * R, (g + 1) * R)
        kc, vc = kc_ref[0, g], vc_ref[0, g]
        s_c = (_dot_nt(stack_q(q_ref, g), kc) * NSA_SCALE).reshape(R, tq, nc)
        p = _masked_softmax(s_c, mask_c[None])
        imp = p.sum(0)
        res_ref[hsl] = gate3(g, 0) * _dot(p.reshape(R * tq, nc).astype(BF), vc).reshape(R, tq, LANES)
        imp_blk = _dot_x3(imp, mb_ref[...])
        cur = qpos // SEL_BLOCK
        vals = jnp.where(lane < cur, imp_blk, NEG)
        vals = jnp.where(lane < nsb, vals, LOWEST)
        picked, _ = _topk_pick(vals, min(N_SEL, nsb), LANES)
        sel = jnp.logical_or(jnp.logical_and(picked, lane < cur), lane == cur)
        selb = jnp.where(sel, 1.0, 0.0).astype(BF)
        qst = stack_q(qr_ref, g)

        reset()

        def sel_tile(j, carry):
            k0 = pl.multiple_of(j * tks, tks)
            kt = ks_ref[pl.ds(k0, tks), gsl]
            vt = vs_ref[pl.ds(k0, tks), gsl]
            kpos = k0 + lax.broadcasted_iota(jnp.int32, (1, tks), 1)
            mask = jnp.logical_and(_dot(selb, e_ref[:, pl.ds(k0, tks)]) > 0.5, kpos <= qpos)
            online(qst, kt, vt, mask)
            return carry

        last_q = q0 + (tq - 1)
        lax.fori_loop(0, last_q // tks + 1, sel_tile, 0)
        res_ref[hsl] += gate3(g, 1) * (acc_ref[...] / jnp.maximum(l_ref[...], TINY))

        reset()

        def win_tile(j, carry):
            k0 = pl.multiple_of(j * tkw, tkw)
            kt = kw_ref[pl.ds(k0, tkw), gsl]
            vt = vw_ref[pl.ds(k0, tkw), gsl]
            kpos = k0 + lax.broadcasted_iota(jnp.int32, (1, tkw), 1)
            mask = jnp.logical_and(kpos <= qpos, kpos > qpos - WINDOW)
            online(qst, kt, vt, mask)
            return carry

        lax.fori_loop(jnp.maximum(q0 - (WINDOW - 1), 0) // tkw, last_q // tkw + 1, win_tile, 0)
        res_ref[hsl] += gate3(g, 2) * (acc_ref[...] / jnp.maximum(l_ref[...], TINY))

    for pb in range(NSA_HEADS // 2):
        o_ref[:, pb * LANES:(pb + 1) * LANES] = jnp.where(lo_half, res_ref[2 * pb], res_ref[2 * pb + 1]).astype(
            o_ref.dtype)


def _nsa_tables(s, nc, nsb):
    c = jnp.arange(nc)[:, None]
    j = jnp.arange(LANES)[None, :]
    per = SEL_BLOCK // CMP_STRIDE
    mb = ((c >= per * j - 1) & (c <= per * j + per - 1) & (c < nc - 1) & (j < nsb)).astype(BF)
    e = (jnp.arange(s)[None, :] // SEL_BLOCK == jnp.arange(LANES)[:, None]).astype(BF)
    return mb, e


def _nsa_prompt(q, qr, gt, cmp, skd, svd, wkd, wvd, n, s):
    tq = _tile(s, NSA_TQ)
    tks = _tile(s, NSA_TK_SEL)
    tkw = _tile(s, NSA_TK_WIN)
    nc = s // CMP_STRIDE
    nsb = s // SEL_BLOCK
    assert nsb <= LANES
    hw = NSA_HEADS * NSA_DH
    mb, e = _nsa_tables(s, nc, nsb)
    nq = s // tq
    row = lambda w: pl.BlockSpec((tq, w), lambda b, i: (b * nq + i, 0))
    seq = pl.BlockSpec((s, 2 * KVW), lambda b, i: (b, 0))
    return pl.pallas_call(
        functools.partial(_nsa_prompt_body, tq=tq, tks=tks, tkw=tkw, nc=nc, nsb=nsb),
        grid=(n, nq),
        in_specs=[row(hw), row(hw), row(LANES),
                  pl.BlockSpec((1, NSA_KV_HEADS, nc, LANES), lambda b, i: (0, b, 0, 0)),
                  pl.BlockSpec((1, NSA_KV_HEADS, nc, LANES), lambda b, i: (1, b, 0, 0)),
                  seq, seq, seq, seq, _full((LANES, s)), _full((nc, LANES))],
        out_specs=row(hw),
        out_shape=SDS((n * s, hw), BF),
        scratch_shapes=[pltpu.VMEM((NSA_HEADS, tq, LANES), F32), pltpu.VMEM((NSA_GROUP, tq, 1), F32),
                        pltpu.VMEM((NSA_GROUP, tq, 1), F32), pltpu.VMEM((NSA_GROUP, tq, LANES), F32)],
        name="nsa_prompt",
    )(q, qr, gt, cmp, cmp, skd, svd, wkd, wvd, e, mb)


def _nsa_cmp_decode_body(pt_ref, q_ref, wpk_ref, wpv_ref, bias_ref, w2_ref, *rest, gp, nchunks, qpos):
    ck_refs, cv_refs = rest[0:gp], rest[gp:2 * gp]
    oc_out, imp_out, abk, abv, rows_k, rows_v = rest[2 * gp:]
    step = pl.program_id(1)
    page = ck_refs[0].shape[3]
    nr = gp * page // CMP_STRIDE
    hw = 2 * CMP_HID
    for refs, wp, ab, rows in ((ck_refs, wpk_ref, abk, rows_k), (cv_refs, wpv_ref, abv, rows_v)):
        for t, ref in enumerate(refs):
            rows[t * page:(t + 1) * page, :] = ref[0, 0].T
        tot = jnp.zeros((nr, NSA_KV_HEADS * hw), F32)
        for rp in range(CMP_STRIDE // 2):
            x = jnp.concatenate([rows[pl.ds(2 * rp, nr, stride=CMP_STRIDE), :],
                                 rows[pl.ds(2 * rp + 1, nr, stride=CMP_STRIDE), :]], axis=1)
            tot = tot + _dot(x.astype(BF), wp[rp])
        ab[pl.ds(pl.multiple_of(step * nr, nr), nr), :] = tot

    @pl.when(step == pl.num_programs(1) - 1)
    def _():
        cidx = lax.broadcasted_iota(jnp.int32, (NSA_GROUP, nchunks), 1)
        mask = (cidx < nchunks - 1) & (cidx * CMP_STRIDE + (CMP_BLOCK - 1) <= qpos)
        for g in range(NSA_KV_HEADS):
            kv = []
            for t, ab in enumerate((abk, abv)):
                a = ab[:, g * hw:g * hw + CMP_HID]
                b = pltpu.roll(ab[:, g * hw + CMP_HID:(g + 1) * hw], nchunks - 1, 0)
                hid = jax.nn.silu(a + b + bias_ref[t])
                kv.append(_dot(hid.astype(BF), w2_ref[t]).astype(BF))
            sl = slice(g * NSA_GROUP, (g + 1) * NSA_GROUP)
            p = _masked_softmax(_dot_nt(q_ref[0, sl], kv[0]) * NSA_SCALE, mask)
            oc_out[0, sl] = _dot(p.astype(BF), kv[1])
            imp_out[0, g:g + 1, :] = jnp.sum(p, axis=0, keepdims=True)


def _nsa_cmp_decode(page_table, li, q16, wpk, wpv, bias, w2, cache_ck, cache_cv, qpos):
    nseq, npages = page_table.shape
    gp = math.gcd(CMP_DECODE_PAGES_PER_STEP, npages)
    page = cache_ck.shape[2]
    nchunks = npages * page // CMP_STRIDE
    ckt, cvt = _feature_major(cache_ck), _feature_major(cache_cv)
    per_seq = lambda shp: pl.BlockSpec((1,) + shp, lambda n, s, pt: (n, 0, 0))
    full3 = lambda shp: pl.BlockSpec(shp, lambda n, s, pt: (0, 0, 0))
    wshape = (CMP_STRIDE // 2, 2 * KVW, NSA_KV_HEADS * 2 * CMP_HID)
    in_specs = [per_seq((NSA_HEADS, NSA_DH)), full3(wshape), full3(wshape), full3((2, 1, CMP_HID)),
                full3((2, CMP_HID, NSA_DH))]
    in_specs += _paged_specs((KVW, page), li, gp) * 2
    ab_shape = pltpu.VMEM((nchunks, NSA_KV_HEADS * 2 * CMP_HID), F32)
    rows_shape = pltpu.VMEM((gp * page, KVW), F32)
    return pl.pallas_call(
        functools.partial(_nsa_cmp_decode_body, gp=gp, nchunks=nchunks, qpos=qpos),
        grid_spec=pltpu.PrefetchScalarGridSpec(
            num_scalar_prefetch=1,
            grid=(nseq, npages // gp),
            in_specs=in_specs,
            out_specs=[per_seq((NSA_HEADS, NSA_DH)), per_seq((NSA_KV_HEADS, nchunks))],
            scratch_shapes=[ab_shape, ab_shape, rows_shape, rows_shape],
        ),
        out_shape=[SDS((nseq, NSA_HEADS, NSA_DH), F32), SDS((nseq, NSA_KV_HEADS, nchunks), F32)],
        name="nsa_cmp_decode",
    )(page_table, q16, wpk, wpv, bias, w2, *([ckt] * gp), *([cvt] * gp))


def _imp_topk_body(imp_ref, mb_ref, idx_ref, *, cur, nsb, width):
    imp_blk = _dot_x3(imp_ref[...], mb_ref[...])
    rows = imp_blk.shape[0]
    lane = lax.broadcasted_iota(jnp.int32, (rows, width), 1)
    vals = jnp.where(lane < cur, imp_blk, NEG)
    vals = jnp.where(lane < nsb, vals, LOWEST)
    _, idxs = _topk_pick(vals, min(N_SEL, nsb), width)
    out = jnp.zeros((rows, LANES), F32)
    lane_o = lax.broadcasted_iota(jnp.int32, (rows, LANES), 1)
    for t, idx in enumerate(idxs):
        out = jnp.where(lane_o == t, idx, out)
    idx_ref[...] = out.astype(jnp.int32)


def _imp_topk(imp, nchunks, nsb, cur):
    rows, n_rows = imp.shape
    width = -(-nsb // LANES) * LANES
    per = SEL_BLOCK // CMP_STRIDE
    c = jnp.arange(n_rows)[:, None]
    j = jnp.arange(width)[None, :]
    mb = ((c >= per * j - 1) & (c <= per * j + per - 1) & (c < nchunks - 1) & (j < nsb)).astype(BF)
    return pl.pallas_call(
        functools.partial(_imp_topk_body, cur=cur, nsb=nsb, width=width),
        grid=(1,),
        in_specs=[_full((rows, n_rows)), _full((n_rows, width))],
        out_specs=_full((rows, LANES)),
        out_shape=SDS((rows, LANES), jnp.int32),
        name="imp_topk",
    )(imp, mb)


def _nsa_sel_decode_body(pt_ref, idx_ref, val_ref, qr_ref, gt_ref, oc_ref, skn_ref, svn_ref, wkn_ref, wvn_ref,
                         wkc_ref, wvc_ref, wink_ref, winv_ref, *rest, nslots, wb, bpp):
    ns = nslots * NSA_KV_HEADS
    selk_refs, selv_refs = rest[0:ns], rest[ns:2 * ns]
    o_out, wink_out, winv_out = rest[2 * ns:]
    n = pl.program_id(0)
    gates = jax.nn.sigmoid(gt_ref[0])
    page = selk_refs[0].shape[3]
    width = nslots * page
    lane = lax.broadcasted_iota(jnp.int32, (1, width), 1)
    slot = lane // page
    blk_in_page = (lane % page) // SEL_BLOCK
    for g in range(NSA_KV_HEADS):
        qg = qr_ref[0, g * NSA_GROUP:(g + 1) * NSA_GROUP]
        qf = qg.astype(F32)

        def attend(kt, vt, mask, k_new, v_new):
            s = jnp.where(mask, _dot(qg, kt) * NSA_SCALE, NEG)
            s_new = jnp.sum(qf * k_new.astype(BF).astype(F32), axis=-1, keepdims=True) * NSA_SCALE
            mx = jnp.maximum(s.max(-1, keepdims=True), s_new)
            e = jnp.where(mask, jnp.exp(s - mx), 0.0)
            e_new = jnp.exp(s_new - mx)
            den = jnp.maximum(e.sum(-1, keepdims=True) + e_new, TINY)
            num = _dot_nt(e.astype(BF), vt) + e_new.astype(BF).astype(F32) * v_new.astype(BF).astype(F32)
            return num / den

        kt = jnp.concatenate([selk_refs[g * nslots + t][0, 0] for t in range(nslots)], axis=1).astype(BF)
        vt = jnp.concatenate([selv_refs[g * nslots + t][0, 0] for t in range(nslots)], axis=1).astype(BF)
        want = jnp.zeros((1, width), jnp.int32)
        valid = jnp.zeros((1, width), jnp.int32)
        for t in range(nslots):
            base = (n * NSA_KV_HEADS + g) * nslots + t
            want = jnp.where(slot == t, idx_ref[base] % bpp, want)
            valid = jnp.where(slot == t, val_ref[base], valid)
        o_s = attend(kt, vt, (valid > 0) & (blk_in_page == want), skn_ref[0, g:g + 1], svn_ref[0, g:g + 1])
        dsl = slice(g * NSA_DH, (g + 1) * NSA_DH)
        kw = wink_ref[0, 0, dsl, :].astype(BF)
        vw = winv_ref[0, 0, dsl, :].astype(BF)
        wmask = lax.broadcasted_iota(jnp.int32, (1, wb), 1) > wb - WINDOW
        o_w = attend(kw, vw, wmask, wkn_ref[0, g:g + 1], wvn_ref[0, g:g + 1])
        sl = slice(g * NSA_GROUP, (g + 1) * NSA_GROUP)
        gg = gates[sl]
        o_out[0, sl] = gg[:, 0:1] * oc_ref[0, sl] + gg[:, 1:2] * o_s + gg[:, 2:3] * o_w
    last = lax.broadcasted_iota(jnp.int32, (KVW, wb), 1) == wb - 1
    for win_ref, col_ref, out_ref in ((wink_ref, wkc_ref, wink_out), (winv_ref, wvc_ref, winv_out)):
        out_ref[0, 0] = jnp.where(last, col_ref[0], pltpu.roll(win_ref[0, 0], wb - 1, 1))


def _nsa_sel_decode(page_table, idx, valid, li, qr16, gt, oc, skn, svn, wkn, wvn, win_k, win_v, cache_sk, cache_sv):
    nseq = page_table.shape[0]
    page = cache_sk.shape[2]
    bpp = page // SEL_BLOCK
    nslots = idx.shape[1]
    idx, valid = idx.reshape(-1), valid.reshape(-1)
    wb = win_k.shape[2]
    skt, svt = _feature_major(cache_sk), _feature_major(cache_sv)
    wkt, wvt = _feature_major(win_k), _feature_major(win_v)
    per_seq = lambda shp: pl.BlockSpec((1,) + shp, lambda n, pt, ix, vl: (n, 0, 0))
    wspec = pl.BlockSpec((1, 1, KVW, wb), lambda n, pt, ix, vl: (li, n, 0, 0))
    sel_specs = []
    for g in range(NSA_KV_HEADS):
        for t in range(nslots):
            sel_specs.append(pl.BlockSpec(
                (1, 1, NSA_DH, page),
                lambda n, pt, ix, vl, g=g, t=t: (li, pt[n, ix[(n * NSA_KV_HEADS + g) * nslots + t] // bpp], g, 0)))
    in_specs = [per_seq((NSA_HEADS, NSA_DH)), per_seq((NSA_HEADS, 3)), per_seq((NSA_HEADS, NSA_DH))]
    in_specs += [per_seq((NSA_KV_HEADS, NSA_DH))] * 4 + [per_seq((KVW, 1))] * 2 + [wspec, wspec] + sel_specs * 2
    ns = nslots * NSA_KV_HEADS
    wout = pl.BlockSpec((1, 1, KVW, wb), lambda n, pt, ix, vl: (0, n, 0, 0))
    o, wk_t, wv_t = pl.pallas_call(
        functools.partial(_nsa_sel_decode_body, nslots=nslots, wb=wb, bpp=bpp),
        grid_spec=pltpu.PrefetchScalarGridSpec(
            num_scalar_prefetch=3,
            grid=(nseq,),
            in_specs=in_specs,
            out_specs=[per_seq((NSA_HEADS, NSA_DH)), wout, wout],
        ),
        out_shape=[SDS((nseq, NSA_HEADS, NSA_DH), F32), SDS((1, nseq, KVW, wb), F32), SDS((1, nseq, KVW, wb), F32)],
        name="nsa_sel_decode",
    )(page_table, idx, valid, qr16, gt, oc, skn, svn, wkn, wvn, wkn.reshape(nseq, KVW, 1), wvn.reshape(nseq, KVW, 1),
      wkt, wvt, *([skt] * ns), *([svt] * ns))
    back = lambda z: jnp.transpose(z.reshape(1, nseq, NSA_KV_HEADS, NSA_DH, wb), (0, 1, 4, 2, 3))
    return o, back(wk_t), back(wv_t)


def _cmp_weights(cmp_pos, w1, w2):
    half = CMP_STRIDE * NSA_DH
    w1ab = jnp.concatenate([w1[:, :half], w1[:, half:]], axis=2).astype(BF)
    pos = jnp.zeros((2, 8, 2 * half), F32).at[:, 0].set(cmp_pos.reshape(2, 2 * half))
    bias = _batched_matmul(pos, w1.astype(BF), F32)[:, 0:1]
    return w1ab, bias


def _cross_sublayer(x, g_pre, g_post, w_q, w_o, mk, mv, layer, n, t):
    q = _norm_matmul(x, g_pre, w_q, BF).reshape(n, t, MEM_HEADS * MEM_DH)
    o = _cross(q, mk, mv, layer).reshape(n * t, MEM_HEADS * MEM_DH)
    return _matmul_norm_res(o, w_o, x, g_post)


def kernel(x_prompt, x_sample, cache_mla_latent, cache_mla_krope, cache_diff_k, cache_diff_v, cache_nsa_cmp_k, cache_nsa_cmp_v, cache_nsa_sel_k, cache_nsa_sel_v, state_nsa_win_k, state_nsa_win_v, cache_mem_k, cache_mem_v, page_table, mem_prompt, norm_g, ffn_w_gu, ffn_w_down, mem_norm_g, mem_w_q, mem_w_k, mem_w_v, mem_w_o, even_w_in, even_w_out, mla_q_norm_g, mla_kv_norm_g, mla_w_uq, mla_w_uk, mla_w_uv, diff_lambda, diff_norm_g, odd_w_in, odd_w_out, nsa_cmp_pos, nsa_cmp_w1, nsa_cmp_w2):
    n, s, d = x_prompt.shape
    nseq, t_dec, _ = x_sample.shape
    assert t_dec == 1
    npages = page_table.shape[1]
    page = cache_mla_latent.shape[2]
    past = npages * page
    depth = norm_g.shape[0]
    mem = mem_prompt.shape[1]
    xp = x_prompt.reshape(n * s, d)
    xs = x_sample.reshape(nseq, d)
    pos_p = jnp.arange(s)
    pos_s = jnp.full((nseq,), past, jnp.int32)
    gain = lambda layer, i: norm_g[layer, i][None, :]

    even_p, even_s, odd_p, odd_s, mem_p = [], [], [], [], []
    for layer in range(depth):
        li = layer // 2
        w_gu = ffn_w_gu[layer].astype(BF)
        w_dn = ffn_w_down[layer].astype(BF)
        wkv = jnp.concatenate([mem_w_k[layer], mem_w_v[layer]], axis=1).astype(BF)
        mkv = _norm_matmul(mem_prompt.reshape(n * mem, d), mem_norm_g[layer][None, :], wkv, F32)
        mw = MEM_HEADS * MEM_DH
        mk_p = mkv[:, :mw].reshape(1, n, mem, MEM_HEADS, MEM_DH)
        mv_p = mkv[:, mw:].reshape(1, n, mem, MEM_HEADS, MEM_DH)
        mem_p.append((mk_p[0], mv_p[0]))

        xp = _ffn(xp, gain(layer, 0), gain(layer, 1), w_gu[0], w_dn[0])
        xs = _ffn(xs, gain(layer, 0), gain(layer, 1), w_gu[0], w_dn[0])

        if layer % 2 == 0:
            lam_init = 0.8 - 0.6 * math.exp(-0.3 * layer)
            ew = _even_weights(even_w_in[li], mla_w_uq[li], mla_w_uk[li], mla_w_uv[li])
            w_out = even_w_out[li].astype(BF)
            qg, kvg = mla_q_norm_g[li][None, :], mla_kv_norm_g[li][None, :]
            dg = jnp.tile(diff_norm_g[li], DIFF_HEADS)[None, :]
            (q, k, v, ckv, kr, dq, dk, dkb, dv, dvb) = _even_proj(xp, gain(layer, 2), qg, kvg, ew, _even_tables(pos_p))
            o_mla = _mla_prompt(q, k, v, n, s)
            o0, o1 = _diff_prompt(dq, dkb, dvb, n, s)
            xp = _even_finish(o_mla, o0, o1, diff_lambda[li], dg, w_out, xp, gain(layer, 3), lam_init)
            even_p.append((ckv.reshape(n, s, KV_LORA), kr[:, :MLA_ROPE].reshape(n, s, MLA_ROPE),
                           dk.reshape(n, s, DIFF_HEADS, 2 * DIFF_DH), dv.reshape(n, s, DIFF_HEADS, DIFF_DV)))
            (q, _, _, ckv, kr, dq, dk, _, dv, _) = _even_proj(xs, gain(layer, 2), qg, kvg, ew, _even_tables(pos_s))
            qabs = _batched_matmul(jnp.transpose(q.reshape(nseq, MLA_HEADS, LANES), (1, 0, 2)), ew["wabs"], BF)
            qd = jnp.pad(jnp.transpose(qabs, (1, 0, 2)), ((0, 0), (0, 16 - MLA_HEADS), (0, 0)))
            nmaps = 2 * DIFF_HEADS
            own = (jnp.arange(nmaps * DIFF_DH)[None, :] // DIFF_DH == jnp.arange(nmaps)[:, None]).astype(dq.dtype)
            dqb = dq[:, None, :] * own[None]
            knew = jnp.concatenate([ckv, kr], axis=1)[:, None, :]
            lat, od = _even_decode(page_table, li, qd, dqb, knew, dk[:, None, :], dv[:, None, :], cache_mla_latent,
                                   cache_mla_krope, cache_diff_k, cache_diff_v)
            lat_h = jnp.transpose(lat[:, :MLA_HEADS], (1, 0, 2))
            o_mla = jnp.transpose(_batched_matmul(lat_h, ew["wuv3"], BF), (1, 0, 2)).reshape(nseq, -1)
            od = od.reshape(nseq, DIFF_HEADS, 2, DIFF_HEADS, DIFF_DV)
            o0 = jnp.stack([od[:, h, 0, h] for h in range(DIFF_HEADS)], axis=1).reshape(nseq, -1)
            o1 = jnp.stack([od[:, h, 1, h] for h in range(DIFF_HEADS)], axis=1).reshape(nseq, -1)
            xs = _even_finish(o_mla, o0, o1, diff_lambda[li], dg, w_out, xs, gain(layer, 3), lam_init)
            even_s.append((ckv.reshape(nseq, 1, KV_LORA), kr[:, :MLA_ROPE].reshape(nseq, 1, MLA_ROPE),
                           dk.reshape(nseq, 1, DIFF_HEADS, 2 * DIFF_DH), dv.reshape(nseq, 1, DIFF_HEADS, DIFF_DV)))
        else:
            w1 = _odd_weights(odd_w_in[li])
            w_out = odd_w_out[li].astype(BF)
            w1ab, bias = _cmp_weights(nsa_cmp_pos[li], nsa_cmp_w1[li], nsa_cmp_w2[li])
            w2 = nsa_cmp_w2[li]
            kv4 = lambda z, m_, t_: z.reshape(m_, t_, NSA_KV_HEADS, NSA_DH)
            c64, s64 = _rope_tables(pos_p, NSA_DH)
            (q, qr, ck, cv, sk, sv, wk, wv, gt, skd, svd, wkd, wvd) = _odd_proj(
                xp, gain(layer, 2), w1, jnp.tile(c64, (1, 2)), jnp.tile(s64, (1, 2)))
            nc = s // CMP_STRIDE
            chunks = lambda z: jnp.transpose(z.reshape(n, nc, CMP_STRIDE, NSA_KV_HEADS, NSA_DH), (0, 3, 1, 2, 4)
                                             ).reshape(n * NSA_KV_HEADS, nc, CMP_STRIDE * NSA_DH)
            ch = jnp.stack([chunks(ck), chunks(cv)]).astype(BF)
            cmp = _cmp_mlp(ch, w1ab, bias, jnp.concatenate([w2, w2], axis=2).astype(BF))
            o = _nsa_prompt(q, qr, gt, cmp, skd, svd, wkd, wvd, n, s)
            xp = _matmul_norm_res(o, w_out, xp, gain(layer, 3))
            win = min(WINDOW, s)
            odd_p.append((kv4(ck, n, s), kv4(cv, n, s), kv4(sk, n, s), kv4(sv, n, s),
                          kv4(wk, n, s)[:, -win:], kv4(wv, n, s)[:, -win:]))
            c64, s64 = _rope_tables(pos_s, NSA_DH)
            (q, qr, ck, cv, sk, sv, wk, wv, gt, _, _, _, _) = _odd_proj(
                xs, gain(layer, 2), w1, jnp.tile(c64, (1, 2)), jnp.tile(s64, (1, 2)))
            wab = w1ab.reshape(2, CMP_STRIDE, 1, NSA_DH, 1, 2 * CMP_HID)
            eye = jnp.eye(NSA_KV_HEADS, dtype=wab.dtype)[None, None, :, None, :, None]
            wp = (wab * eye).reshape(2, CMP_STRIDE // 2, 2 * KVW, NSA_KV_HEADS * 2 * CMP_HID)
            oc, imp = _nsa_cmp_decode(page_table, li, q.reshape(nseq, NSA_HEADS, NSA_DH), wp[0], wp[1], bias,
                                      w2.astype(BF), cache_nsa_cmp_k, cache_nsa_cmp_v, past)
            nchunks = past // CMP_STRIDE
            nsb = -(-(past + 1) // SEL_BLOCK)
            cur = past // SEL_BLOCK
            ksel = min(N_SEL, nsb)
            top = _imp_topk(imp.reshape(nseq * NSA_KV_HEADS, -1), nchunks, nsb, cur)[:, :ksel]
            valid = (top < cur).astype(jnp.int32)
            idx = jnp.where(top < cur, top, 0)
            gt3 = gt[:, :3 * NSA_HEADS].reshape(nseq, NSA_HEADS, 3)
            g2 = lambda z: z.reshape(nseq, NSA_KV_HEADS, NSA_DH)
            o, wk_new, wv_new = _nsa_sel_decode(
                page_table, idx, valid, li, qr.reshape(nseq, NSA_HEADS, NSA_DH), gt3, oc, g2(sk), g2(sv), g2(wk),
                g2(wv), state_nsa_win_k, state_nsa_win_v, cache_nsa_sel_k, cache_nsa_sel_v)
            xs = _matmul_norm_res(o.reshape(nseq, NSA_HEADS * NSA_DH), w_out, xs, gain(layer, 3))
            odd_s.append((kv4(ck, nseq, 1), kv4(cv, nseq, 1), kv4(sk, nseq, 1), kv4(sv, nseq, 1),
                          wk_new[0], wv_new[0]))

        w_q = mem_w_q[layer].astype(BF)
        w_o = mem_w_o[layer].astype(BF)
        xp = _cross_sublayer(xp, gain(layer, 4), gain(layer, 5), w_q, w_o, mk_p, mv_p, 0, n, s)
        xs = _cross_sublayer(xs, gain(layer, 4), gain(layer, 5), w_q, w_o, cache_mem_k, cache_mem_v, layer, nseq, 1)
        xp = _ffn(xp, gain(layer, 6), gain(layer, 7), w_gu[1], w_dn[1])
        xs = _ffn(xs, gain(layer, 6), gain(layer, 7), w_gu[1], w_dn[1])

    st = lambda lst, i: jnp.stack([e[i] for e in lst])
    return (xp.reshape(n, s, d), xs.reshape(nseq, 1, d),
            st(even_p, 0), st(even_p, 1), st(even_p, 2), st(even_p, 3),
            st(odd_p, 0), st(odd_p, 1), st(odd_p, 2), st(odd_p, 3), st(odd_p, 4), st(odd_p, 5),
            st(mem_p, 0), st(mem_p, 1),
            st(even_s, 0), st(even_s, 1), st(even_s, 2), st(even_s, 3),
            st(odd_s, 0), st(odd_s, 1), st(odd_s, 2), st(odd_s, 3), st(odd_s, 4), st(odd_s, 5))
```

```python
import functools
import math

import jax
import jax.numpy as jnp
from jax import lax
from jax.experimental import pallas as pl
from jax.experimental.pallas import tpu as pltpu

BF = jnp.bfloat16
F32 = jnp.float32
SDS = jax.ShapeDtypeStruct

MLA_HEADS, MLA_NOPE, MLA_ROPE, MLA_DV = 12, 64, 32, 64
Q_LORA, KV_LORA = 384, 256
DIFF_HEADS, DIFF_DH, DIFF_DV = 4, 32, 64
NSA_HEADS, NSA_KV_HEADS, NSA_DH = 16, 2, 64
NSA_GROUP = NSA_HEADS // NSA_KV_HEADS
CMP_STRIDE, CMP_BLOCK, CMP_HID = 16, 32, 128
SEL_BLOCK, N_SEL, WINDOW = 64, 15, 512
MEM_HEADS, MEM_DH = 4, 128
HALF_STEP = 0.5
ROPE_THETA = 10000.0
RMS_EPS = 1e-6
NEG = -1e30
TINY = 1e-30
LOWEST = -3e38
MLA_SCALE = (MLA_NOPE + MLA_ROPE) ** -0.5
DIFF_SCALE = DIFF_DH ** -0.5
NSA_SCALE = NSA_DH ** -0.5
MEM_SCALE = MEM_DH ** -0.5
LOG2E = math.log2(math.e)
LANES = 128
EVEN_DECODE_PAGES_PER_STEP = 16
CMP_DECODE_PAGES_PER_STEP = 32
NSA_TQ, NSA_TK_SEL = 256, 1024
DIFF_TQ, DIFF_TK = 256, 1024
FFN_TF = 256


def _dot(a, b):
    return jnp.dot(a, b, preferred_element_type=F32)


def _dot_nt(a, b):
    return lax.dot_general(a, b, (((1,), (1,)), ((), ())), preferred_element_type=F32)


def _dot_x3(x, w):
    h1 = x.astype(BF)
    r1 = x - h1.astype(F32)
    h2 = r1.astype(BF)
    h3 = (r1 - h2.astype(F32)).astype(BF)
    return _dot(h1, w) + _dot(h2, w) + _dot(h3, w)


def _rms(x, g):
    return x * lax.rsqrt(jnp.mean(x * x, axis=-1, keepdims=True) + RMS_EPS) * g


def _masked_softmax(s, mask):
    s = jnp.where(mask, s, NEG)
    e = jnp.where(mask, jnp.exp(s - s.max(-1, keepdims=True)), 0.0)
    return e / jnp.maximum(e.sum(-1, keepdims=True), TINY)


def _tile(m, pref):
    if m <= pref:
        return m
    t = pref
    while m % t:
        t //= 2
    return t


def _full(shape):
    n = len(shape)
    return pl.BlockSpec(shape, lambda *a: (0,) * n)


def _ffn_body(x_ref, gpre_ref, gpost_ref, wg_ref, wu_ref, wd_ref, o_ref, xn_ref, acc_ref, *, nf):
    j = pl.program_id(1)

    @pl.when(j == 0)
    def _():
        xn_ref[...] = _rms(x_ref[...], gpre_ref[...]).astype(BF)
        acc_ref[...] = jnp.zeros_like(acc_ref)

    xn = xn_ref[...]
    g = _dot(xn, wg_ref[...])
    u = _dot(xn, wu_ref[...])
    h = (jax.nn.silu(g) * u).astype(BF)
    acc_ref[...] += _dot(h, wd_ref[...])

    @pl.when(j == nf - 1)
    def _():
        o_ref[...] = x_ref[...] + HALF_STEP * _rms(acc_ref[...], gpost_ref[...])


def _ffn(x, gpre, gpost, w_gu, w_down):
    m, d = x.shape
    f = w_down.shape[0]
    tf = FFN_TF if f % FFN_TF == 0 else 256
    nf = f // tf
    tm = _tile(m, 1024)
    return pl.pallas_call(
        functools.partial(_ffn_body, nf=nf),
        grid=(m // tm, nf),
        in_specs=[
            pl.BlockSpec((tm, d), lambda i, j: (i, 0)),
            pl.BlockSpec((1, d), lambda i, j: (0, 0)),
            pl.BlockSpec((1, d), lambda i, j: (0, 0)),
            pl.BlockSpec((d, tf), lambda i, j: (0, j)),
            pl.BlockSpec((d, tf), lambda i, j: (0, j + nf)),
            pl.BlockSpec((tf, d), lambda i, j: (j, 0)),
        ],
        out_specs=pl.BlockSpec((tm, d), lambda i, j: (i, 0)),
        out_shape=SDS((m, d), F32),
        scratch_shapes=[pltpu.VMEM((tm, d), BF), pltpu.VMEM((tm, d), F32)],
        name="ffn",
    )(x, gpre, gpost, w_gu, w_gu, w_down)


def _norm_matmul_body(x_ref, g_ref, w_ref, o_ref, xn_ref):
    @pl.when(pl.program_id(1) == 0)
    def _():
        xn_ref[...] = _rms(x_ref[...], g_ref[...]).astype(BF)

    o_ref[...] = _dot(xn_ref[...], w_ref[...]).astype(o_ref.dtype)


def _norm_matmul(x, g, w, out_dtype):
    m, d = x.shape
    n = w.shape[1]
    tm = _tile(m, 512)
    tn = _tile(n, 512)
    return pl.pallas_call(
        _norm_matmul_body,
        grid=(m // tm, n // tn),
        in_specs=[
            pl.BlockSpec((tm, d), lambda i, j: (i, 0)),
            pl.BlockSpec((1, d), lambda i, j: (0, 0)),
            pl.BlockSpec((d, tn), lambda i, j: (0, j)),
        ],
        out_specs=pl.BlockSpec((tm, tn), lambda i, j: (i, j)),
        out_shape=SDS((m, n), out_dtype),
        scratch_shapes=[pltpu.VMEM((tm, d), BF)],
        name="norm_matmul",
    )(x, g, w)


def _matmul_norm_res_body(a_ref, w_ref, x_ref, g_ref, o_ref):
    y = _dot(a_ref[...].astype(BF), w_ref[...])
    o_ref[...] = x_ref[...] + _rms(y, g_ref[...])


def _matmul_norm_res(a, w, x, g):
    m, k = a.shape
    d = w.shape[1]
    tm = _tile(m, 512)
    return pl.pallas_call(
        _matmul_norm_res_body,
        grid=(m // tm,),
        in_specs=[
            pl.BlockSpec((tm, k), lambda i: (i, 0)),
            pl.BlockSpec((k, d), lambda i: (0, 0)),
            pl.BlockSpec((tm, d), lambda i: (i, 0)),
            pl.BlockSpec((1, d), lambda i: (0, 0)),
        ],
        out_specs=pl.BlockSpec((tm, d), lambda i: (i, 0)),
        out_shape=SDS((m, d), F32),
        name="matmul_norm_res",
    )(a, w, x, g)


def _batched_matmul_body(a_ref, w_ref, o_ref):
    o_ref[0] = _dot(a_ref[0].astype(BF), w_ref[0]).astype(o_ref.dtype)


def _batched_matmul(a, w, out_dtype):
    b, m, k = a.shape
    n = w.shape[2]
    return pl.pallas_call(
        _batched_matmul_body,
        grid=(b,),
        in_specs=[pl.BlockSpec((1, m, k), lambda i: (i, 0, 0)), pl.BlockSpec((1, k, n), lambda i: (i, 0, 0))],
        out_specs=pl.BlockSpec((1, m, n), lambda i: (i, 0, 0)),
        out_shape=SDS((b, m, n), out_dtype),
        name="batched_matmul",
    )(a, w)


def _rope_tables(pos, d):
    half = d // 2
    inv = ROPE_THETA ** (-jnp.arange(half, dtype=F32) * (2.0 / d))
    ang = pos.astype(F32)[:, None] * inv[None, :]
    c, s = jnp.cos(ang), jnp.sin(ang)
    return jnp.concatenate([c, c], -1), jnp.concatenate([s, s], -1)


def _rot_cols(w, d):
    k, n = w.shape
    wr = w.reshape(k, n // d, 2, d // 2)
    return jnp.concatenate([-wr[:, :, 1], wr[:, :, 0]], axis=2).reshape(k, n)


def _pad_cols(w, n):
    return jnp.pad(w, ((0, 0), (0, n - w.shape[1])))


E_CQ, E_CKV, E_DV, E_DQ, E_DK, E_KR = 0, 384, 640, 896, 1152, 1408
E_DQR, E_DKR, E_KRR, E_W = 1536, 1792, 2048, 2176
QW = MLA_HEADS * LANES


def _even_weights(w_in, w_uq, w_uk, w_uv):
    dqk = DIFF_HEADS * 2 * DIFF_DH
    o = [0, Q_LORA, Q_LORA + KV_LORA, Q_LORA + KV_LORA + MLA_ROPE]
    c_q, c_kv, k_r = w_in[:, o[0]:o[1]], w_in[:, o[1]:o[2]], w_in[:, o[2]:o[3]]
    dq = w_in[:, o[3]:o[3] + dqk]
    dk = w_in[:, o[3] + dqk:o[3] + 2 * dqk]
    dv = w_in[:, o[3] + 2 * dqk:]
    w1 = jnp.concatenate([c_q, c_kv, dv, dq, dk, _pad_cols(k_r, LANES), _rot_cols(dq, DIFF_DH),
                          _rot_cols(dk, DIFF_DH), _pad_cols(_rot_cols(k_r, MLA_ROPE), LANES)], axis=1).astype(BF)
    uq = w_uq.reshape(Q_LORA, MLA_HEADS, MLA_NOPE + MLA_ROPE)
    z = jnp.zeros((Q_LORA, MLA_HEADS, LANES - MLA_NOPE - MLA_ROPE), w_uq.dtype)
    wq = jnp.concatenate([uq, z], axis=2).reshape(Q_LORA, QW).astype(BF)
    rot = _rot_cols(uq[:, :, MLA_NOPE:].reshape(Q_LORA, -1), MLA_ROPE).reshape(Q_LORA, MLA_HEADS, MLA_ROPE)
    wqr = jnp.concatenate([jnp.zeros_like(uq[:, :, :MLA_NOPE]), rot, z], axis=2).reshape(Q_LORA, QW).astype(BF)
    uk = w_uk.reshape(KV_LORA, MLA_HEADS, MLA_NOPE)
    wk_lat = jnp.concatenate([uk, jnp.zeros((KV_LORA, MLA_HEADS, LANES - MLA_NOPE), w_uk.dtype)], axis=2)
    wk_lat = wk_lat.reshape(KV_LORA, QW).astype(BF)
    place = jnp.zeros((LANES, LANES), F32).at[jnp.arange(MLA_ROPE), MLA_NOPE + jnp.arange(MLA_ROPE)].set(1.0)
    wk_rope = jnp.tile(place, (1, MLA_HEADS)).astype(BF)
    wabs = jnp.zeros((MLA_HEADS, LANES, KV_LORA + LANES), F32)
    wabs = wabs.at[:, :MLA_NOPE, :KV_LORA].set(jnp.transpose(uk, (1, 2, 0)))
    wabs = wabs.at[:, MLA_NOPE + jnp.arange(MLA_ROPE), KV_LORA + jnp.arange(MLA_ROPE)].set(1.0)
    wuv3 = jnp.transpose(w_uv.reshape(KV_LORA, MLA_HEADS, MLA_DV), (1, 0, 2)).astype(BF)
    return dict(w1=w1, wq=wq, wqr=wqr, wk_lat=wk_lat, wk_rope=wk_rope, wuv=w_uv.astype(BF),
                wabs=wabs.astype(BF), wuv3=wuv3)


def _even_tables(pos):
    c32, s32 = _rope_tables(pos, MLA_ROPE)
    t = pos.shape[0]
    cos256 = jnp.tile(c32, (1, 8))
    sin256 = jnp.tile(s32, (1, 8))
    cosq = jnp.concatenate([jnp.ones((t, MLA_NOPE), F32), c32, jnp.zeros((t, 32), F32)], axis=1)
    sinq = jnp.concatenate([jnp.zeros((t, MLA_NOPE), F32), s32, jnp.zeros((t, 32), F32)], axis=1)
    return cos256, sin256, cosq, sinq


def _even_proj_body(x_ref, g_ref, qg_ref, kvg_ref, w1_ref, wq_ref, wqr_ref, wkl_ref, wkr_ref, wuv_ref,
                    c256_ref, s256_ref, cq_ref, sq_ref,
                    q_out, k_out, v_out, ckv_out, kr_out, dq_out, dk_out, dkb_out, dv_out, dvb_out):
    xn = _rms(x_ref[...], g_ref[...]).astype(BF)
    proj = _dot(xn, w1_ref[...])
    c256, s256 = c256_ref[...], s256_ref[...]
    dq = proj[:, E_DQ:E_DQ + 256] * c256 + proj[:, E_DQR:E_DQR + 256] * s256
    dk = proj[:, E_DK:E_DK + 256] * c256 + proj[:, E_DKR:E_DKR + 256] * s256
    kr = proj[:, E_KR:E_KR + LANES] * c256[:, :LANES] + proj[:, E_KRR:E_KRR + LANES] * s256[:, :LANES]
    dv = proj[:, E_DV:E_DV + 256]
    cqn = _rms(proj[:, E_CQ:E_CQ + Q_LORA], qg_ref[...]).astype(BF)
    ckv = _rms(proj[:, E_CKV:E_CKV + KV_LORA], kvg_ref[...])
    ckv_b = ckv.astype(BF)
    q = _dot(cqn, wq_ref[...])
    qr = _dot(cqn, wqr_ref[...])
    cq, sq = cq_ref[...], sq_ref[...]
    for h in range(MLA_HEADS):
        sl = slice(h * LANES, (h + 1) * LANES)
        q_out[:, sl] = (q[:, sl] * cq + qr[:, sl] * sq).astype(BF)
    k_out[...] = (_dot(ckv_b, wkl_ref[...]) + _dot(kr.astype(BF), wkr_ref[...])).astype(BF)
    v_out[...] = _dot(ckv_b, wuv_ref[...]).astype(BF)
    ckv_out[...] = ckv
    kr_out[...] = kr
    dq_out[...] = dq.astype(BF)
    dk_out[...] = dk
    dkb_out[...] = dk.astype(BF)
    dv_out[...] = dv
    dvb_out[...] = dv.astype(BF)


def _even_proj(x, g, q_g, kv_g, ew, tables):
    m, d = x.shape
    tm = _tile(m, 256)
    nt = tables[0].shape[0] // tm
    row = lambda w: pl.BlockSpec((tm, w), lambda i: (i, 0))
    tab = lambda w: pl.BlockSpec((tm, w), lambda i: (i % nt, 0))
    vw = MLA_HEADS * MLA_DV
    return pl.pallas_call(
        _even_proj_body,
        grid=(m // tm,),
        in_specs=[row(d), _full((1, d)), _full((1, Q_LORA)), _full((1, KV_LORA)), _full((d, E_W)),
                  _full((Q_LORA, QW)), _full((Q_LORA, QW)), _full((KV_LORA, QW)), _full((LANES, QW)),
                  _full((KV_LORA, vw)), tab(256), tab(256), tab(LANES), tab(LANES)],
        out_specs=[row(QW), row(QW), row(vw), row(KV_LORA), row(LANES), row(256), row(256), row(256), row(256),
                   row(256)],
        out_shape=[SDS((m, QW), BF), SDS((m, QW), BF), SDS((m, vw), BF), SDS((m, KV_LORA), F32),
                   SDS((m, LANES), F32), SDS((m, 256), BF), SDS((m, 256), F32), SDS((m, 256), BF),
                   SDS((m, 256), F32), SDS((m, 256), BF)],
        name="even_proj",
    )(x, g, q_g, kv_g, ew["w1"], ew["wq"], ew["wqr"], ew["wk_lat"], ew["wk_rope"], ew["wuv"], *tables)


def _mla_prompt_body(q_ref, k_ref, v_ref, o_ref, m_ref, l_ref, acc_ref, *, tq):
    i = pl.program_id(2)
    j = pl.program_id(3)
    lo_half = lax.broadcasted_iota(jnp.int32, (tq, LANES), 1) < MLA_DV

    @pl.when(j == 0)
    def _():
        m_ref[...] = jnp.full_like(m_ref, NEG)
        l_ref[...] = jnp.zeros_like(l_ref)
        acc_ref[...] = jnp.zeros_like(acc_ref)

    def step(diag):
        v = v_ref[...]
        if diag:
            mask = (lax.broadcasted_iota(jnp.int32, (tq, tq), 1) <= lax.broadcasted_iota(jnp.int32, (tq, tq), 0))
        for a in range(2):
            sl = slice(a * LANES, (a + 1) * LANES)
            s = _dot_nt(q_ref[:, sl], k_ref[:, sl]) * (MLA_SCALE * LOG2E)
            if diag:
                s = jnp.where(mask, s, NEG)
            m_old = m_ref[a]
            m_new = jnp.maximum(m_old, s.max(-1, keepdims=True))
            corr = jnp.exp2(m_old - m_new)
            p = jnp.exp2(s - m_new)
            l_ref[a] = l_ref[a] * corr + p.sum(-1, keepdims=True)
            m_ref[a] = m_new
            acc_ref[a] = acc_ref[a] * corr + _dot(p.astype(BF), v)

    @pl.when(j < i)
    def _():
        step(False)

    @pl.when(j == i)
    def _():
        step(True)
        o_ref[...] = jnp.where(lo_half, acc_ref[0] / l_ref[0], acc_ref[1] / l_ref[1]).astype(o_ref.dtype)


def _mla_prompt(q, k, v, n, s):
    tq = _tile(s, 1024)
    nq = s // tq
    pairs = MLA_HEADS // 2
    return pl.pallas_call(
        functools.partial(_mla_prompt_body, tq=tq),
        grid=(n, pairs, nq, nq),
        in_specs=[
            pl.BlockSpec((tq, 2 * LANES), lambda b, p, i, j: (b * nq + i, p)),
            pl.BlockSpec((tq, 2 * LANES), lambda b, p, i, j: (b * nq + jnp.minimum(j, i), p)),
            pl.BlockSpec((tq, LANES), lambda b, p, i, j: (b * nq + jnp.minimum(j, i), p)),
        ],
        out_specs=pl.BlockSpec((tq, LANES), lambda b, p, i, j: (b * nq + i, p)),
        out_shape=SDS((n * s, MLA_HEADS * MLA_DV), BF),
        scratch_shapes=[pltpu.VMEM((2, tq, 1), F32), pltpu.VMEM((2, tq, 1), F32), pltpu.VMEM((2, tq, LANES), F32)],
        name="mla_prompt",
    )(q, k, v)


def _diff_prompt_body(q_ref, k_ref, v_ref, o0_ref, o1_ref, m_ref, l_ref, acc_ref, *, tq, tk):
    q0 = pl.program_id(1) * tq
    dw = DIFF_HEADS * DIFF_DV
    lane = lax.broadcasted_iota(jnp.int32, (tq, dw), 1)
    lo_half = lax.broadcasted_iota(jnp.int32, (tq, LANES), 1) < DIFF_DV
    nmaps = 2 * DIFF_HEADS
    q = q_ref[...]
    qms = [jnp.where((lane >= idx * DIFF_DH) & (lane < (idx + 1) * DIFF_DH), q, jnp.zeros_like(q))
           for idx in range(nmaps)]
    m_ref[...] = jnp.full_like(m_ref, NEG)
    l_ref[...] = jnp.zeros_like(l_ref)
    acc_ref[...] = jnp.zeros_like(acc_ref)

    def tile(j, masked):
        k0 = pl.multiple_of(j * tk, tk)
        kt = k_ref[pl.ds(k0, tk), :]
        if masked:
            mask = (k0 + lax.broadcasted_iota(jnp.int32, (1, tk), 1)) <= (q0 + lax.broadcasted_iota(jnp.int32, (tq, 1), 0))
        for idx in range(nmaps):
            h = idx // 2
            s = _dot_nt(qms[idx], kt) * (DIFF_SCALE * LOG2E)
            if masked:
                s = jnp.where(mask, s, NEG)
            m_old = m_ref[idx]
            m_new = jnp.maximum(m_old, s.max(-1, keepdims=True))
            corr = jnp.exp2(m_old - m_new)
            p = jnp.exp2(s - m_new)
            l_ref[idx] = l_ref[idx] * corr + p.sum(-1, keepdims=True)
            m_ref[idx] = m_new
            vt = v_ref[pl.ds(k0, tk), (h // 2) * LANES:(h // 2 + 1) * LANES]
            acc_ref[idx] = acc_ref[idx] * corr + _dot(p.astype(BF), vt)

    def full_tile(j, carry):
        tile(j, False)
        return carry

    jd = q0 // tk
    lax.fori_loop(0, jd, full_tile, 0)
    tile(jd, True)
    for mm, o_ref in ((0, o0_ref), (1, o1_ref)):
        for pb in range(DIFF_HEADS // 2):
            a, b = (2 * pb) * 2 + mm, (2 * pb + 1) * 2 + mm
            o_ref[:, pb * LANES:(pb + 1) * LANES] = jnp.where(lo_half, acc_ref[a] / l_ref[a], acc_ref[b] / l_ref[b])


def _diff_prompt(dq, dk, dv, n, s):
    tq = _tile(s, DIFF_TQ)
    tk = _tile(s, DIFF_TK)
    assert tk % tq == 0
    nq = s // tq
    nmaps = 2 * DIFF_HEADS
    dw = DIFF_HEADS * DIFF_DV
    row = pl.BlockSpec((tq, dw), lambda b, i: (b * nq + i, 0))
    seq = pl.BlockSpec((s, dw), lambda b, i: (b, 0))
    return pl.pallas_call(
        functools.partial(_diff_prompt_body, tq=tq, tk=tk),
        grid=(n, nq),
        in_specs=[row, seq, seq],
        out_specs=[row, row],
        out_shape=[SDS((n * s, dw), F32), SDS((n * s, dw), F32)],
        scratch_shapes=[pltpu.VMEM((nmaps, tq, 1), F32), pltpu.VMEM((nmaps, tq, 1), F32),
                        pltpu.VMEM((nmaps, tq, LANES), F32)],
        name="diff_prompt",
    )(dq, dk, dv)


def _even_finish_body(om_ref, o0_ref, o1_ref, lam_ref, dg_ref, wm_ref, wd_ref, x_ref, g_ref, o_ref, *, lam_init):
    lp = lam_ref[...]
    lam = (jnp.exp(jnp.sum(lp[0:1] * lp[1:2], axis=-1, keepdims=True))
           - jnp.exp(jnp.sum(lp[2:3] * lp[3:4], axis=-1, keepdims=True)) + lam_init)
    d = o0_ref[...] - lam * o1_ref[...]
    tm = d.shape[0]
    lane = lax.broadcasted_iota(jnp.int32, (tm, 256), 1)
    d2 = d * d
    inv = jnp.zeros_like(d)
    for h in range(DIFF_HEADS):
        hmask = (lane >= h * DIFF_DV) & (lane < (h + 1) * DIFF_DV)
        ms = jnp.sum(jnp.where(hmask, d2, 0.0), axis=-1, keepdims=True) * (1.0 / DIFF_DV)
        inv = jnp.where(hmask, lax.rsqrt(ms + RMS_EPS), inv)
    dn = (d * inv * dg_ref[...]) * (1.0 - lam_init)
    y = _dot(om_ref[...], wm_ref[...]) + _dot(dn.astype(BF), wd_ref[...])
    o_ref[...] = x_ref[...] + _rms(y, g_ref[...])


def _even_finish(o_mla, o0, o1, lam_p, diff_g256, w_out, x, g, lam_init):
    m, d = x.shape
    tm = _tile(m, 512)
    vw = MLA_HEADS * MLA_DV
    row = lambda w: pl.BlockSpec((tm, w), lambda i: (i, 0))
    return pl.pallas_call(
        functools.partial(_even_finish_body, lam_init=lam_init),
        grid=(m // tm,),
        in_specs=[row(vw), row(256), row(256), _full((4, DIFF_DH)), _full((1, 256)), _full((vw, d)),
                  _full((256, d)), row(d), _full((1, d))],
        out_specs=row(d),
        out_shape=SDS((m, d), F32),
        name="even_finish",
    )(o_mla, o0, o1, lam_p, diff_g256, w_out[:vw], w_out[vw:], x, g)


def _cross_body(q_ref, mk_ref, mv_ref, o_ref):
    for h in range(MEM_HEADS):
        sl = slice(h * MEM_DH, (h + 1) * MEM_DH)
        k = mk_ref[0, 0, :, h, :].astype(BF)
        v = mv_ref[0, 0, :, h, :].astype(BF)
        s = _dot_nt(q_ref[0, :, sl], k) * MEM_SCALE
        p = jax.nn.softmax(s, axis=-1)
        o_ref[0, :, sl] = _dot(p.astype(BF), v).astype(o_ref.dtype)


def _cross(q, mk, mv, layer):
    n, t, w = q.shape
    mem = mk.shape[2]
    tq = _tile(t, 512)
    mspec = pl.BlockSpec((1, 1, mem, MEM_HEADS, MEM_DH), lambda b, i: (layer, b, 0, 0, 0))
    return pl.pallas_call(
        _cross_body,
        grid=(n, t // tq),
        in_specs=[pl.BlockSpec((1, tq, w), lambda b, i: (b, i, 0)), mspec, mspec],
        out_specs=pl.BlockSpec((1, tq, w), lambda b, i: (b, i, 0)),
        out_shape=SDS((n, t, w), BF),
        name="cross",
    )(q, mk, mv)


def _even_decode_body(pt_ref, qd_ref, dq_ref, knew_ref, dkn_ref, dvn_ref, *rest, gp):
    lat_refs, kr_refs = rest[0:gp], rest[gp:2 * gp]
    dk_refs, dv_refs = rest[2 * gp:3 * gp], rest[3 * gp:4 * gp]
    lat_out, od_out, m1, l1, acc1, m2, l2, acc2 = rest[4 * gp:]
    step = pl.program_id(1)
    qd = qd_ref[0]
    dq = dq_ref[0]

    @pl.when(step == 0)
    def _():
        kn = knew_ref[0].astype(BF)
        s1 = jnp.sum(qd.astype(F32) * kn.astype(F32), axis=-1, keepdims=True) * MLA_SCALE
        m1[...] = s1
        l1[...] = jnp.ones_like(l1)
        acc1[...] = jnp.broadcast_to(kn[:, :KV_LORA].astype(F32), acc1.shape)
        s2 = jnp.sum(dq.astype(F32) * dkn_ref[0].astype(BF).astype(F32), axis=-1, keepdims=True) * DIFF_SCALE
        m2[...] = s2
        l2[...] = jnp.ones_like(l2)
        acc2[...] = jnp.broadcast_to(dvn_ref[0].astype(BF).astype(F32), acc2.shape)

    c = jnp.concatenate([r[0, 0] for r in lat_refs], axis=0).astype(BF)
    krt = jnp.concatenate([r[0, 0] for r in kr_refs], axis=1).astype(BF)
    s = (_dot_nt(qd[:, :KV_LORA], c) + _dot(qd[:, KV_LORA:KV_LORA + MLA_ROPE], krt)) * MLA_SCALE
    m_new = jnp.maximum(m1[...], s.max(-1, keepdims=True))
    corr = jnp.exp(m1[...] - m_new)
    p = jnp.exp(s - m_new)
    l1[...] = l1[...] * corr + p.sum(-1, keepdims=True)
    acc1[...] = acc1[...] * corr + _dot(p.astype(BF), c)
    m1[...] = m_new

    kt = jnp.concatenate([r[0, 0] for r in dk_refs], axis=1).astype(BF)
    vt = jnp.concatenate([r[0, 0] for r in dv_refs], axis=1).astype(BF)
    s = _dot(dq, kt) * DIFF_SCALE
    m_new = jnp.maximum(m2[...], s.max(-1, keepdims=True))
    corr = jnp.exp(m2[...] - m_new)
    p = jnp.exp(s - m_new)
    l2[...] = l2[...] * corr + p.sum(-1, keepdims=True)
    acc2[...] = acc2[...] * corr + _dot_nt(p.astype(BF), vt)
    m2[...] = m_new

    @pl.when(step == pl.num_programs(1) - 1)
    def _():
        lat_out[0] = acc1[...] / l1[...]
        od_out[0] = acc2[...] / l2[...]


def _paged_specs(shape_tail, li, gp):
    nd = len(shape_tail)
    specs = []
    for t in range(gp):
        specs.append(pl.BlockSpec((1, 1) + shape_tail,
                                  lambda n, s, pt, t=t: (li, pt[n, s * gp + t]) + (0,) * nd))
    return specs


def _feature_major(cache):
    l, pool, rows = cache.shape[:3]
    nd = cache.ndim
    return jnp.transpose(cache, (0, 1) + tuple(range(3, nd)) + (2,)).reshape(l, pool, -1, rows)


def _even_decode(page_table, li, qd, dqb, knew, dkn, dvn, cache_lat, cache_kr, cache_dk, cache_dv):
    nseq, npages = page_table.shape
    gp = math.gcd(EVEN_DECODE_PAGES_PER_STEP, npages)
    page = cache_lat.shape[2]
    hp = qd.shape[1]
    nmaps = 2 * DIFF_HEADS
    dw = DIFF_HEADS * DIFF_DV
    krt, dkt, dvt = _feature_major(cache_kr), _feature_major(cache_dk), _feature_major(cache_dv)
    per_seq = lambda shp: pl.BlockSpec((1,) + shp, lambda n, s, pt: (n, 0, 0))
    in_specs = [per_seq((hp, KV_LORA + LANES)), per_seq((nmaps, dw)), per_seq((1, KV_LORA + LANES)),
                per_seq((1, dw)), per_seq((1, dw))]
    in_specs += _paged_specs((page, KV_LORA), li, gp) + _paged_specs((MLA_ROPE, page), li, gp)
    in_specs += _paged_specs((dw, page), li, gp) * 2
    return pl.pallas_call(
        functools.partial(_even_decode_body, gp=gp),
        grid_spec=pltpu.PrefetchScalarGridSpec(
            num_scalar_prefetch=1,
            grid=(nseq, npages // gp),
            in_specs=in_specs,
            out_specs=[per_seq((hp, KV_LORA)), per_seq((nmaps, dw))],
            scratch_shapes=[pltpu.VMEM((hp, 1), F32), pltpu.VMEM((hp, 1), F32), pltpu.VMEM((hp, KV_LORA), F32),
                            pltpu.VMEM((nmaps, 1), F32), pltpu.VMEM((nmaps, 1), F32),
                            pltpu.VMEM((nmaps, dw), F32)],
        ),
        out_shape=[SDS((nseq, hp, KV_LORA), F32), SDS((nseq, nmaps, dw), F32)],
        name="even_decode",
    )(page_table, qd, dqb, knew, dkn, dvn, *([cache_lat] * gp), *([krt] * gp), *([dkt] * gp), *([dvt] * gp))


O_Q, O_QR, O_CK, O_CV, O_SK, O_SKR, O_SV, O_WK, O_WKR, O_WV, O_GT = (
    0, 1024, 2048, 2176, 2304, 2432, 2560, 2688, 2816, 2944, 3072)
O_SKD, O_SKDR, O_SVD, O_WKD, O_WKDR, O_WVD, O_W = 3200, 3456, 3712, 3968, 4224, 4480, 4736
KVW = NSA_KV_HEADS * NSA_DH


def _dup_cols(w):
    k = w.shape[0]
    wr = w.reshape(k, NSA_KV_HEADS, 1, NSA_DH)
    return jnp.broadcast_to(wr, (k, NSA_KV_HEADS, 2, NSA_DH)).reshape(k, 2 * KVW)


def _odd_weights(w_in):
    hw = NSA_HEADS * NSA_DH
    q = w_in[:, :hw]
    ck, cv, sk, sv, wk, wv = [w_in[:, hw + t * KVW: hw + (t + 1) * KVW] for t in range(6)]
    gl = w_in[:, hw + 6 * KVW:]
    skr, wkr = _rot_cols(sk, NSA_DH), _rot_cols(wk, NSA_DH)
    return jnp.concatenate([q, _rot_cols(q, NSA_DH), ck, cv, sk, skr, sv, wk, wkr, wv, _pad_cols(gl, LANES),
                            _dup_cols(sk), _dup_cols(skr), _dup_cols(sv), _dup_cols(wk), _dup_cols(wkr),
                            _dup_cols(wv)], axis=1).astype(BF)


def _odd_proj_body(x_ref, g_ref, w_ref, cos_ref, sin_ref,
                   q_out, qr_out, ck_out, cv_out, sk_out, sv_out, wk_out, wv_out, gt_out,
                   skd_out, svd_out, wkd_out, wvd_out):
    xn = _rms(x_ref[...], g_ref[...]).astype(BF)
    proj = _dot(xn, w_ref[...])
    cos, sin = cos_ref[...], sin_ref[...]
    for c in range(NSA_HEADS * NSA_DH // LANES):
        sl = slice(c * LANES, (c + 1) * LANES)
        q = proj[:, O_Q + c * LANES:O_Q + (c + 1) * LANES]
        q_out[:, sl] = q.astype(BF)
        qr_out[:, sl] = (q * cos + proj[:, O_QR + c * LANES:O_QR + (c + 1) * LANES] * sin).astype(BF)
    ck_out[...] = proj[:, O_CK:O_CK + KVW]
    cv_out[...] = proj[:, O_CV:O_CV + KVW]
    sk_out[...] = proj[:, O_SK:O_SK + KVW] * cos + proj[:, O_SKR:O_SKR + KVW] * sin
    sv_out[...] = proj[:, O_SV:O_SV + KVW]
    wk_out[...] = proj[:, O_WK:O_WK + KVW] * cos + proj[:, O_WKR:O_WKR + KVW] * sin
    wv_out[...] = proj[:, O_WV:O_WV + KVW]
    gt_out[...] = proj[:, O_GT:O_GT + LANES]
    for g in range(NSA_KV_HEADS):
        sl = slice(g * LANES, (g + 1) * LANES)
        skd_out[:, sl] = (proj[:, O_SKD + g * LANES:O_SKD + (g + 1) * LANES] * cos
                          + proj[:, O_SKDR + g * LANES:O_SKDR + (g + 1) * LANES] * sin).astype(BF)
        wkd_out[:, sl] = (proj[:, O_WKD + g * LANES:O_WKD + (g + 1) * LANES] * cos
                          + proj[:, O_WKDR + g * LANES:O_WKDR + (g + 1) * LANES] * sin).astype(BF)
    svd_out[...] = proj[:, O_SVD:O_SVD + 2 * KVW].astype(BF)
    wvd_out[...] = proj[:, O_WVD:O_WVD + 2 * KVW].astype(BF)


def _odd_proj(x, g, w1, cos, sin):
    m, d = x.shape
    tm = _tile(m, 256)
    nt = cos.shape[0] // tm
    hw = NSA_HEADS * NSA_DH
    row = lambda w: pl.BlockSpec((tm, w), lambda i: (i, 0))
    tab = pl.BlockSpec((tm, LANES), lambda i: (i % nt, 0))
    return pl.pallas_call(
        _odd_proj_body,
        grid=(m // tm,),
        in_specs=[row(d), _full((1, d)), _full((d, O_W)), tab, tab],
        out_specs=[row(hw), row(hw)] + [row(KVW)] * 6 + [row(LANES)] + [row(2 * KVW)] * 4,
        out_shape=[SDS((m, hw), BF), SDS((m, hw), BF)] + [SDS((m, KVW), F32)] * 6 + [SDS((m, LANES), F32)]
        + [SDS((m, 2 * KVW), BF)] * 4,
        name="odd_proj",
    )(x, g, w1, cos, sin)


def _cmp_mlp_body(ch_ref, w1_ref, bias_ref, w2_ref, o_ref):
    ab = _dot(ch_ref[0, 0], w1_ref[0])
    nc = ab.shape[0]
    a = ab[:, :CMP_HID]
    b = pltpu.roll(ab[:, CMP_HID:], nc - 1, 0)
    hid = jax.nn.silu(a + b + bias_ref[0])
    o_ref[0, 0] = _dot(hid.astype(BF), w2_ref[0]).astype(o_ref.dtype)


def _cmp_mlp(ch, w1ab, bias, w2dup):
    _, b, nc, kk = ch.shape
    return pl.pallas_call(
        _cmp_mlp_body,
        grid=(2, b),
        in_specs=[pl.BlockSpec((1, 1, nc, kk), lambda t, i: (t, i, 0, 0)),
                  pl.BlockSpec((1, kk, 2 * CMP_HID), lambda t, i: (t, 0, 0)),
                  pl.BlockSpec((1, 1, CMP_HID), lambda t, i: (t, 0, 0)),
                  pl.BlockSpec((1, CMP_HID, LANES), lambda t, i: (t, 0, 0))],
        out_specs=pl.BlockSpec((1, 1, nc, LANES), lambda t, i: (t, i, 0, 0)),
        out_shape=SDS((2, b, nc, LANES), BF),
        name="cmp_mlp",
    )(ch, w1ab, bias, w2dup)


def _topk_pick(vals, k, width):
    rows = vals.shape[0]
    lane_f = lax.broadcasted_iota(jnp.int32, (rows, width), 1).astype(F32)
    picked = jnp.zeros((rows, width), jnp.bool_)
    idxs = []
    for _ in range(k):
        mx = vals.max(-1, keepdims=True)
        idx = jnp.min(jnp.where(vals == mx, lane_f, float(width)), axis=-1, keepdims=True)
        hit = lane_f == idx
        picked = jnp.logical_or(picked, hit)
        vals = jnp.where(hit, LOWEST, vals)
        idxs.append(idx)
    return picked, idxs


def _nsa_prompt_body(q_ref, qr_ref, gt_ref, kc_ref, vc_ref, ks_ref, vs_ref, kw_ref, vw_ref, e_ref, mb_ref, o_ref,
                     res_ref, m_ref, l_ref, acc_ref, *, tq, tks, tkw, nc, nsb):
    i = pl.program_id(1)
    q0 = i * tq
    qpos = q0 + lax.broadcasted_iota(jnp.int32, (tq, 1), 0)
    lane = lax.broadcasted_iota(jnp.int32, (tq, LANES), 1)
    lo_half = lane < NSA_DH
    gates = jax.nn.sigmoid(gt_ref[...])
    c_end = lax.broadcasted_iota(jnp.int32, (tq, nc), 1) * CMP_STRIDE + (CMP_BLOCK - 1)
    mask_c = c_end <= qpos
    zero_b = jnp.zeros((tq, LANES), BF)
    R = NSA_GROUP

    def stack_q(ref, g):
        parts = []
        for r in range(R):
            hd = g * R + r
            qh = ref[:, (hd // 2) * LANES:(hd // 2 + 1) * LANES]
            parts.append(jnp.where(lo_half if hd % 2 == 0 else jnp.logical_not(lo_half), qh, zero_b))
        return jnp.concatenate(parts, axis=0)

    def gate3(g, b):
        return jnp.stack([gates[:, 3 * (g * R + r) + b:3 * (g * R + r) + b + 1] for r in range(R)], axis=0)

    def online(qst, kt, vt, mask):
        for r in range(R):
            s = _dot_nt(qst[r * tq:(r + 1) * tq], kt) * (NSA_SCALE * LOG2E)
            s = jnp.where(mask, s, NEG)
            m_old = m_ref[r]
            m_new = jnp.maximum(m_old, s.max(-1, keepdims=True))
            corr = jnp.exp2(m_old - m_new)
            p = jnp.exp2(s - m_new)
            l_ref[r] = l_ref[r] * corr + p.sum(-1, keepdims=True)
            acc_ref[r] = acc_ref[r] * corr + _dot(p.astype(BF), vt)
            m_ref[r] = m_new

    def reset():
        m_ref[...] = jnp.full_like(m_ref, NEG)
        l_ref[...] = jnp.zeros_like(l_ref)
        acc_ref[...] = jnp.zeros_like(acc_ref)

    for g in range(NSA_KV_HEADS):
        gsl = slice(g * LANES, (g + 1) * LANES)
        hsl = slice(g * R, (g + 1) * R)
        kc, vc = kc_ref[0, g], vc_ref[0, g]
        s_c = (_dot_nt(stack_q(q_ref, g), kc) * NSA_SCALE).reshape(R, tq, nc)
        p = _masked_softmax(s_c, mask_c[None])
        imp = p.sum(0)
        res_ref[hsl] = gate3(g, 0) * _dot(p.reshape(R * tq, nc).astype(BF), vc).reshape(R, tq, LANES)
        imp_blk = _dot_x3(imp, mb_ref[...])
        cur = qpos // SEL_BLOCK
        vals = jnp.where(lane < cur, imp_blk, NEG)
        vals = jnp.where(lane < nsb, vals, LOWEST)
        picked, _ = _topk_pick(vals, min(N_SEL, nsb), LANES)
        sel = jnp.logical_or(jnp.logical_and(picked, lane < cur), lane == cur)
        selb = jnp.where(sel, 1.0, 0.0).astype(BF)
        qst = stack_q(qr_ref, g)

        reset()

        def sel_tile(j, carry):
            k0 = pl.multiple_of(j * tks, tks)
            kt = ks_ref[pl.ds(k0, tks), gsl]
            vt = vs_ref[pl.ds(k0, tks), gsl]
            kpos = k0 + lax.broadcasted_iota(jnp.int32, (1, tks), 1)
            mask = jnp.logical_and(_dot(selb, e_ref[:, pl.ds(k0, tks)]) > 0.5, kpos <= qpos)
            online(qst, kt, vt, mask)
            return carry

        last_q = q0 + (tq - 1)
        lax.fori_loop(0, last_q // tks + 1, sel_tile, 0)
        res_ref[hsl] += gate3(g, 1) * (acc_ref[...] / jnp.maximum(l_ref[...], TINY))

        k0 = pl.multiple_of(jnp.maximum(q0 + tq - tkw, 0), tq)
        kt = kw_ref[pl.ds(k0, tkw), gsl]
        vt = vw_ref[pl.ds(k0, tkw), gsl]
        kpos = k0 + lax.broadcasted_iota(jnp.int32, (1, tkw), 1)
        mask = jnp.logical_and(kpos <= qpos, kpos > qpos - WINDOW)
        for r in range(R):
            s = jnp.where(mask, _dot_nt(qst[r * tq:(r + 1) * tq], kt) * (NSA_SCALE * LOG2E), NEG)
            p = jnp.exp2(s - s.max(-1, keepdims=True))
            o_w = _dot(p.astype(BF), vt) / p.sum(-1, keepdims=True)
            hd = g * R + r
            res_ref[hd] += gates[:, 3 * hd + 2:3 * hd + 3] * o_w

    for pb in range(NSA_HEADS // 2):
        o_ref[:, pb * LANES:(pb + 1) * LANES] = jnp.where(lo_half, res_ref[2 * pb], res_ref[2 * pb + 1]).astype(
            o_ref.dtype)


def _nsa_tables(s, nc, nsb):
    c = jnp.arange(nc)[:, None]
    j = jnp.arange(LANES)[None, :]
    per = SEL_BLOCK // CMP_STRIDE
    mb = ((c >= per * j - 1) & (c <= per * j + per - 1) & (c < nc - 1) & (j < nsb)).astype(BF)
    e = (jnp.arange(s)[None, :] // SEL_BLOCK == jnp.arange(LANES)[:, None]).astype(BF)
    return mb, e


def _nsa_prompt(q, qr, gt, cmp, skd, svd, wkd, wvd, n, s):
    tq = _tile(s, NSA_TQ)
    tks = _tile(s, NSA_TK_SEL)
    tkw = min(WINDOW + tq, s)
    nc = s // CMP_STRIDE
    nsb = s // SEL_BLOCK
    assert nsb <= LANES
    hw = NSA_HEADS * NSA_DH
    mb, e = _nsa_tables(s, nc, nsb)
    nq = s // tq
    row = lambda w: pl.BlockSpec((tq, w), lambda b, i: (b * nq + i, 0))
    seq = pl.BlockSpec((s, 2 * KVW), lambda b, i: (b, 0))
    return pl.pallas_call(
        functools.partial(_nsa_prompt_body, tq=tq, tks=tks, tkw=tkw, nc=nc, nsb=nsb),
        grid=(n, nq),
        in_specs=[row(hw), row(hw), row(LANES),
                  pl.BlockSpec((1, NSA_KV_HEADS, nc, LANES), lambda b, i: (0, b, 0, 0)),
                  pl.BlockSpec((1, NSA_KV_HEADS, nc, LANES), lambda b, i: (1, b, 0, 0)),
                  seq, seq, seq, seq, _full((LANES, s)), _full((nc, LANES))],
        out_specs=row(hw),
        out_shape=SDS((n * s, hw), BF),
        scratch_shapes=[pltpu.VMEM((NSA_HEADS, tq, LANES), F32), pltpu.VMEM((NSA_GROUP, tq, 1), F32),
                        pltpu.VMEM((NSA_GROUP, tq, 1), F32), pltpu.VMEM((NSA_GROUP, tq, LANES), F32)],
        name="nsa_prompt",
    )(q, qr, gt, cmp, cmp, skd, svd, wkd, wvd, e, mb)


def _nsa_cmp_decode_body(pt_ref, q_ref, wpk_ref, wpv_ref, bias_ref, w2_ref, *rest, gp, nchunks, qpos):
    ck_refs, cv_refs = rest[0:gp], rest[gp:2 * gp]
    oc_out, imp_out, abk, abv, rows_k, rows_v = rest[2 * gp:]
    step = pl.program_id(1)
    page = ck_refs[0].shape[3]
    nr = gp * page // CMP_STRIDE
    hw = 2 * CMP_HID
    for refs, wp, ab, rows in ((ck_refs, wpk_ref, abk, rows_k), (cv_refs, wpv_ref, abv, rows_v)):
        for t, ref in enumerate(refs):
            rows[t * page:(t + 1) * page, :] = ref[0, 0].T
        x = jnp.concatenate([rows[pl.ds(r, nr, stride=CMP_STRIDE), :].astype(BF) for r in range(CMP_STRIDE)], axis=1)
        ab[pl.ds(pl.multiple_of(step * nr, nr), nr), :] = _dot(x, wp[...])

    @pl.when(step == pl.num_programs(1) - 1)
    def _():
        cidx = lax.broadcasted_iota(jnp.int32, (NSA_GROUP, nchunks), 1)
        mask = (cidx < nchunks - 1) & (cidx * CMP_STRIDE + (CMP_BLOCK - 1) <= qpos)
        for g in range(NSA_KV_HEADS):
            kv = []
            for t, ab in enumerate((abk, abv)):
                a = ab[:, g * hw:g * hw + CMP_HID]
                b = pltpu.roll(ab[:, g * hw + CMP_HID:(g + 1) * hw], nchunks - 1, 0)
                hid = jax.nn.silu(a + b + bias_ref[t])
                kv.append(_dot(hid.astype(BF), w2_ref[t]).astype(BF))
            sl = slice(g * NSA_GROUP, (g + 1) * NSA_GROUP)
            p = _masked_softmax(_dot_nt(q_ref[0, sl], kv[0]) * NSA_SCALE, mask)
            oc_out[0, sl] = _dot(p.astype(BF), kv[1])
            imp_out[0, g:g + 1, :] = jnp.sum(p, axis=0, keepdims=True)


def _nsa_cmp_decode(page_table, li, q16, wpk, wpv, bias, w2, cache_ck, cache_cv, qpos):
    nseq, npages = page_table.shape
    gp = math.gcd(CMP_DECODE_PAGES_PER_STEP, npages)
    page = cache_ck.shape[2]
    nchunks = npages * page // CMP_STRIDE
    ckt, cvt = _feature_major(cache_ck), _feature_major(cache_cv)
    per_seq = lambda shp: pl.BlockSpec((1,) + shp, lambda n, s, pt: (n, 0, 0))
    full3 = lambda shp: pl.BlockSpec(shp, lambda n, s, pt: (0, 0, 0))
    wspec = pl.BlockSpec((CMP_STRIDE * KVW, NSA_KV_HEADS * 2 * CMP_HID), lambda n, s, pt: (0, 0))
    in_specs = [per_seq((NSA_HEADS, NSA_DH)), wspec, wspec, full3((2, 1, CMP_HID)),
                full3((2, CMP_HID, NSA_DH))]
    in_specs += _paged_specs((KVW, page), li, gp) * 2
    ab_shape = pltpu.VMEM((nchunks, NSA_KV_HEADS * 2 * CMP_HID), F32)
    rows_shape = pltpu.VMEM((gp * page, KVW), F32)
    return pl.pallas_call(
        functools.partial(_nsa_cmp_decode_body, gp=gp, nchunks=nchunks, qpos=qpos),
        grid_spec=pltpu.PrefetchScalarGridSpec(
            num_scalar_prefetch=1,
            grid=(nseq, npages // gp),
            in_specs=in_specs,
            out_specs=[per_seq((NSA_HEADS, NSA_DH)), per_seq((NSA_KV_HEADS, nchunks))],
            scratch_shapes=[ab_shape, ab_shape, rows_shape, rows_shape],
        ),
        out_shape=[SDS((nseq, NSA_HEADS, NSA_DH), F32), SDS((nseq, NSA_KV_HEADS, nchunks), F32)],
        name="nsa_cmp_decode",
    )(page_table, q16, wpk, wpv, bias, w2, *([ckt] * gp), *([cvt] * gp))


def _imp_topk_body(imp_ref, mb_ref, idx_ref, *, cur, nsb, width):
    imp_blk = _dot_x3(imp_ref[...], mb_ref[...])
    rows = imp_blk.shape[0]
    lane = lax.broadcasted_iota(jnp.int32, (rows, width), 1)
    vals = jnp.where(lane < cur, imp_blk, NEG)
    vals = jnp.where(lane < nsb, vals, LOWEST)
    _, idxs = _topk_pick(vals, min(N_SEL, nsb), width)
    out = jnp.zeros((rows, LANES), F32)
    lane_o = lax.broadcasted_iota(jnp.int32, (rows, LANES), 1)
    for t, idx in enumerate(idxs):
        out = jnp.where(lane_o == t, idx, out)
    idx_ref[...] = out.astype(jnp.int32)


def _imp_topk(imp, nchunks, nsb, cur):
    rows, n_rows = imp.shape
    width = -(-nsb // LANES) * LANES
    per = SEL_BLOCK // CMP_STRIDE
    c = jnp.arange(n_rows)[:, None]
    j = jnp.arange(width)[None, :]
    mb = ((c >= per * j - 1) & (c <= per * j + per - 1) & (c < nchunks - 1) & (j < nsb)).astype(BF)
    return pl.pallas_call(
        functools.partial(_imp_topk_body, cur=cur, nsb=nsb, width=width),
        grid=(1,),
        in_specs=[_full((rows, n_rows)), _full((n_rows, width))],
        out_specs=_full((rows, LANES)),
        out_shape=SDS((rows, LANES), jnp.int32),
        name="imp_topk",
    )(imp, mb)


def _nsa_sel_decode_body(pt_ref, idx_ref, val_ref, qr_ref, gt_ref, oc_ref, skn_ref, svn_ref, wkn_ref, wvn_ref,
                         wkc_ref, wvc_ref, wink_ref, winv_ref, *rest, nslots, wb, bpp):
    ns = nslots * NSA_KV_HEADS
    selk_refs, selv_refs = rest[0:ns], rest[ns:2 * ns]
    o_out, wink_out, winv_out = rest[2 * ns:]
    n = pl.program_id(0)
    gates = jax.nn.sigmoid(gt_ref[0])
    page = selk_refs[0].shape[3]
    width = nslots * page
    lane = lax.broadcasted_iota(jnp.int32, (1, width), 1)
    slot = lane // page
    blk_in_page = (lane % page) // SEL_BLOCK
    for g in range(NSA_KV_HEADS):
        qg = qr_ref[0, g * NSA_GROUP:(g + 1) * NSA_GROUP]
        qf = qg.astype(F32)

        def attend(kt, vt, mask, k_new, v_new):
            s = jnp.where(mask, _dot(qg, kt) * NSA_SCALE, NEG)
            s_new = jnp.sum(qf * k_new.astype(BF).astype(F32), axis=-1, keepdims=True) * NSA_SCALE
            mx = jnp.maximum(s.max(-1, keepdims=True), s_new)
            e = jnp.where(mask, jnp.exp(s - mx), 0.0)
            e_new = jnp.exp(s_new - mx)
            den = jnp.maximum(e.sum(-1, keepdims=True) + e_new, TINY)
            num = _dot_nt(e.astype(BF), vt) + e_new.astype(BF).astype(F32) * v_new.astype(BF).astype(F32)
            return num / den

        kt = jnp.concatenate([selk_refs[g * nslots + t][0, 0] for t in range(nslots)], axis=1).astype(BF)
        vt = jnp.concatenate([selv_refs[g * nslots + t][0, 0] for t in range(nslots)], axis=1).astype(BF)
        want = jnp.zeros((1, width), jnp.int32)
        valid = jnp.zeros((1, width), jnp.int32)
        for t in range(nslots):
            base = (n * NSA_KV_HEADS + g) * nslots + t
            want = jnp.where(slot == t, idx_ref[base] % bpp, want)
            valid = jnp.where(slot == t, val_ref[base], valid)
        o_s = attend(kt, vt, (valid > 0) & (blk_in_page == want), skn_ref[0, g:g + 1], svn_ref[0, g:g + 1])
        dsl = slice(g * NSA_DH, (g + 1) * NSA_DH)
        kw = wink_ref[0, 0, dsl, :].astype(BF)
        vw = winv_ref[0, 0, dsl, :].astype(BF)
        wmask = lax.broadcasted_iota(jnp.int32, (1, wb), 1) > wb - WINDOW
        o_w = attend(kw, vw, wmask, wkn_ref[0, g:g + 1], wvn_ref[0, g:g + 1])
        sl = slice(g * NSA_GROUP, (g + 1) * NSA_GROUP)
        gg = gates[sl]
        o_out[0, sl] = gg[:, 0:1] * oc_ref[0, sl] + gg[:, 1:2] * o_s + gg[:, 2:3] * o_w
    last = lax.broadcasted_iota(jnp.int32, (KVW, wb), 1) == wb - 1
    for win_ref, col_ref, out_ref in ((wink_ref, wkc_ref, wink_out), (winv_ref, wvc_ref, winv_out)):
        out_ref[0, 0] = jnp.where(last, col_ref[0], pltpu.roll(win_ref[0, 0], wb - 1, 1))


def _nsa_sel_decode(page_table, idx, valid, li, qr16, gt, oc, skn, svn, wkn, wvn, win_k, win_v, cache_sk, cache_sv):
    nseq = page_table.shape[0]
    page = cache_sk.shape[2]
    bpp = page // SEL_BLOCK
    nslots = idx.shape[1]
    idx, valid = idx.reshape(-1), valid.reshape(-1)
    wb = win_k.shape[2]
    skt, svt = _feature_major(cache_sk), _feature_major(cache_sv)
    wkt, wvt = _feature_major(win_k), _feature_major(win_v)
    per_seq = lambda shp: pl.BlockSpec((1,) + shp, lambda n, pt, ix, vl: (n, 0, 0))
    wspec = pl.BlockSpec((1, 1, KVW, wb), lambda n, pt, ix, vl: (li, n, 0, 0))
    sel_specs = []
    for g in range(NSA_KV_HEADS):
        for t in range(nslots):
            sel_specs.append(pl.BlockSpec(
                (1, 1, NSA_DH, page),
                lambda n, pt, ix, vl, g=g, t=t: (li, pt[n, ix[(n * NSA_KV_HEADS + g) * nslots + t] // bpp], g, 0)))
    in_specs = [per_seq((NSA_HEADS, NSA_DH)), per_seq((NSA_HEADS, 3)), per_seq((NSA_HEADS, NSA_DH))]
    in_specs += [per_seq((NSA_KV_HEADS, NSA_DH))] * 4 + [per_seq((KVW, 1))] * 2 + [wspec, wspec] + sel_specs * 2
    ns = nslots * NSA_KV_HEADS
    wout = pl.BlockSpec((1, 1, KVW, wb), lambda n, pt, ix, vl: (0, n, 0, 0))
    o, wk_t, wv_t = pl.pallas_call(
        functools.partial(_nsa_sel_decode_body, nslots=nslots, wb=wb, bpp=bpp),
        grid_spec=pltpu.PrefetchScalarGridSpec(
            num_scalar_prefetch=3,
            grid=(nseq,),
            in_specs=in_specs,
            out_specs=[per_seq((NSA_HEADS, NSA_DH)), wout, wout],
        ),
        out_shape=[SDS((nseq, NSA_HEADS, NSA_DH), F32), SDS((1, nseq, KVW, wb), F32), SDS((1, nseq, KVW, wb), F32)],
        name="nsa_sel_decode",
    )(page_table, idx, valid, qr16, gt, oc, skn, svn, wkn, wvn, wkn.reshape(nseq, KVW, 1), wvn.reshape(nseq, KVW, 1),
      wkt, wvt, *([skt] * ns), *([svt] * ns))
    back = lambda z: jnp.transpose(z.reshape(1, nseq, NSA_KV_HEADS, NSA_DH, wb), (0, 1, 4, 2, 3))
    return o, back(wk_t), back(wv_t)


def _cmp_weights(cmp_pos, w1, w2):
    half = CMP_STRIDE * NSA_DH
    w1ab = jnp.concatenate([w1[:, :half], w1[:, half:]], axis=2).astype(BF)
    pos = jnp.zeros((2, 8, 2 * half), F32).at[:, 0].set(cmp_pos.reshape(2, 2 * half))
    bias = _batched_matmul(pos, w1.astype(BF), F32)[:, 0:1]
    return w1ab, bias


def _cross_sublayer(x, g_pre, g_post, w_q, w_o, mk, mv, layer, n, t):
    q = _norm_matmul(x, g_pre, w_q, BF).reshape(n, t, MEM_HEADS * MEM_DH)
    o = _cross(q, mk, mv, layer).reshape(n * t, MEM_HEADS * MEM_DH)
    return _matmul_norm_res(o, w_o, x, g_post)


def kernel(x_prompt, x_sample, cache_mla_latent, cache_mla_krope, cache_diff_k, cache_diff_v, cache_nsa_cmp_k, cache_nsa_cmp_v, cache_nsa_sel_k, cache_nsa_sel_v, state_nsa_win_k, state_nsa_win_v, cache_mem_k, cache_mem_v, page_table, mem_prompt, norm_g, ffn_w_gu, ffn_w_down, mem_norm_g, mem_w_q, mem_w_k, mem_w_v, mem_w_o, even_w_in, even_w_out, mla_q_norm_g, mla_kv_norm_g, mla_w_uq, mla_w_uk, mla_w_uv, diff_lambda, diff_norm_g, odd_w_in, odd_w_out, nsa_cmp_pos, nsa_cmp_w1, nsa_cmp_w2):
    n, s, d = x_prompt.shape
    nseq, t_dec, _ = x_sample.shape
    assert t_dec == 1
    npages = page_table.shape[1]
    page = cache_mla_latent.shape[2]
    past = npages * page
    depth = norm_g.shape[0]
    mem = mem_prompt.shape[1]
    xp = x_prompt.reshape(n * s, d)
    xs = x_sample.reshape(nseq, d)
    pos_p = jnp.arange(s)
    pos_s = jnp.full((nseq,), past, jnp.int32)
    gain = lambda layer, i: norm_g[layer, i][None, :]

    even_p, even_s, odd_p, odd_s, mem_p = [], [], [], [], []
    for layer in range(depth):
        li = layer // 2
        w_gu = ffn_w_gu[layer].astype(BF)
        w_dn = ffn_w_down[layer].astype(BF)
        wkv = jnp.concatenate([mem_w_k[layer], mem_w_v[layer]], axis=1).astype(BF)
        mkv = _norm_matmul(mem_prompt.reshape(n * mem, d), mem_norm_g[layer][None, :], wkv, F32)
        mw = MEM_HEADS * MEM_DH
        mk_p = mkv[:, :mw].reshape(1, n, mem, MEM_HEADS, MEM_DH)
        mv_p = mkv[:, mw:].reshape(1, n, mem, MEM_HEADS, MEM_DH)
        mem_p.append((mk_p[0], mv_p[0]))

        xp = _ffn(xp, gain(layer, 0), gain(layer, 1), w_gu[0], w_dn[0])
        xs = _ffn(xs, gain(layer, 0), gain(layer, 1), w_gu[0], w_dn[0])

        if layer % 2 == 0:
            lam_init = 0.8 - 0.6 * math.exp(-0.3 * layer)
            ew = _even_weights(even_w_in[li], mla_w_uq[li], mla_w_uk[li], mla_w_uv[li])
            w_out = even_w_out[li].astype(BF)
            qg, kvg = mla_q_norm_g[li][None, :], mla_kv_norm_g[li][None, :]
            dg = jnp.tile(diff_norm_g[li], DIFF_HEADS)[None, :]
            (q, k, v, ckv, kr, dq, dk, dkb, dv, dvb) = _even_proj(xp, gain(layer, 2), qg, kvg, ew, _even_tables(pos_p))
            o_mla = _mla_prompt(q, k, v, n, s)
            o0, o1 = _diff_prompt(dq, dkb, dvb, n, s)
            xp = _even_finish(o_mla, o0, o1, diff_lambda[li], dg, w_out, xp, gain(layer, 3), lam_init)
            even_p.append((ckv.reshape(n, s, KV_LORA), kr[:, :MLA_ROPE].reshape(n, s, MLA_ROPE),
                           dk.reshape(n, s, DIFF_HEADS, 2 * DIFF_DH), dv.reshape(n, s, DIFF_HEADS, DIFF_DV)))
            (q, _, _, ckv, kr, dq, dk, _, dv, _) = _even_proj(xs, gain(layer, 2), qg, kvg, ew, _even_tables(pos_s))
            qabs = _batched_matmul(jnp.transpose(q.reshape(nseq, MLA_HEADS, LANES), (1, 0, 2)), ew["wabs"], BF)
            qd = jnp.pad(jnp.transpose(qabs, (1, 0, 2)), ((0, 0), (0, 16 - MLA_HEADS), (0, 0)))
            nmaps = 2 * DIFF_HEADS
            own = (jnp.arange(nmaps * DIFF_DH)[None, :] // DIFF_DH == jnp.arange(nmaps)[:, None]).astype(dq.dtype)
            dqb = dq[:, None, :] * own[None]
            knew = jnp.concatenate([ckv, kr], axis=1)[:, None, :]
            lat, od = _even_decode(page_table, li, qd, dqb, knew, dk[:, None, :], dv[:, None, :], cache_mla_latent,
                                   cache_mla_krope, cache_diff_k, cache_diff_v)
            lat_h = jnp.transpose(lat[:, :MLA_HEADS], (1, 0, 2))
            o_mla = jnp.transpose(_batched_matmul(lat_h, ew["wuv3"], BF), (1, 0, 2)).reshape(nseq, -1)
            od = od.reshape(nseq, DIFF_HEADS, 2, DIFF_HEADS, DIFF_DV)
            o0 = jnp.stack([od[:, h, 0, h] for h in range(DIFF_HEADS)], axis=1).reshape(nseq, -1)
            o1 = jnp.stack([od[:, h, 1, h] for h in range(DIFF_HEADS)], axis=1).reshape(nseq, -1)
            xs = _even_finish(o_mla, o0, o1, diff_lambda[li], dg, w_out, xs, gain(layer, 3), lam_init)
            even_s.append((ckv.reshape(nseq, 1, KV_LORA), kr[:, :MLA_ROPE].reshape(nseq, 1, MLA_ROPE),
                           dk.reshape(nseq, 1, DIFF_HEADS, 2 * DIFF_DH), dv.reshape(nseq, 1, DIFF_HEADS, DIFF_DV)))
        else:
            w1 = _odd_weights(odd_w_in[li])
            w_out = odd_w_out[li].astype(BF)
            w1ab, bias = _cmp_weights(nsa_cmp_pos[li], nsa_cmp_w1[li], nsa_cmp_w2[li])
            w2 = nsa_cmp_w2[li]
            kv4 = lambda z, m_, t_: z.reshape(m_, t_, NSA_KV_HEADS, NSA_DH)
            c64, s64 = _rope_tables(pos_p, NSA_DH)
            (q, qr, ck, cv, sk, sv, wk, wv, gt, skd, svd, wkd, wvd) = _odd_proj(
                xp, gain(layer, 2), w1, jnp.tile(c64, (1, 2)), jnp.tile(s64, (1, 2)))
            nc = s // CMP_STRIDE
            chunks = lambda z: jnp.transpose(z.reshape(n, nc, CMP_STRIDE, NSA_KV_HEADS, NSA_DH), (0, 3, 1, 2, 4)
                                             ).reshape(n * NSA_KV_HEADS, nc, CMP_STRIDE * NSA_DH)
            ch = jnp.stack([chunks(ck), chunks(cv)]).astype(BF)
            cmp = _cmp_mlp(ch, w1ab, bias, jnp.concatenate([w2, w2], axis=2).astype(BF))
            o = _nsa_prompt(q, qr, gt, cmp, skd, svd, wkd, wvd, n, s)
            xp = _matmul_norm_res(o, w_out, xp, gain(layer, 3))
            win = min(WINDOW, s)
            odd_p.append((kv4(ck, n, s), kv4(cv, n, s), kv4(sk, n, s), kv4(sv, n, s),
                          kv4(wk, n, s)[:, -win:], kv4(wv, n, s)[:, -win:]))
            c64, s64 = _rope_tables(pos_s, NSA_DH)
            (q, qr, ck, cv, sk, sv, wk, wv, gt, _, _, _, _) = _odd_proj(
                xs, gain(layer, 2), w1, jnp.tile(c64, (1, 2)), jnp.tile(s64, (1, 2)))
            wab = w1ab.reshape(2, CMP_STRIDE, 1, NSA_DH, 1, 2 * CMP_HID)
            eye = jnp.eye(NSA_KV_HEADS, dtype=wab.dtype)[None, None, :, None, :, None]
            wp = (wab * eye).reshape(2, CMP_STRIDE * KVW, NSA_KV_HEADS * 2 * CMP_HID)
            oc, imp = _nsa_cmp_decode(page_table, li, q.reshape(nseq, NSA_HEADS, NSA_DH), wp[0], wp[1], bias,
                                      w2.astype(BF), cache_nsa_cmp_k, cache_nsa_cmp_v, past)
            nchunks = past // CMP_STRIDE
            nsb = -(-(past + 1) // SEL_BLOCK)
            cur = past // SEL_BLOCK
            ksel = min(N_SEL, nsb)
            top = _imp_topk(imp.reshape(nseq * NSA_KV_HEADS, -1), nchunks, nsb, cur)[:, :ksel]
            valid = (top < cur).astype(jnp.int32)
            idx = jnp.where(top < cur, top, 0)
            gt3 = gt[:, :3 * NSA_HEADS].reshape(nseq, NSA_HEADS, 3)
            g2 = lambda z: z.reshape(nseq, NSA_KV_HEADS, NSA_DH)
            o, wk_new, wv_new = _nsa_sel_decode(
                page_table, idx, valid, li, qr.reshape(nseq, NSA_HEADS, NSA_DH), gt3, oc, g2(sk), g2(sv), g2(wk),
                g2(wv), state_nsa_win_k, state_nsa_win_v, cache_nsa_sel_k, cache_nsa_sel_v)
            xs = _matmul_norm_res(o.reshape(nseq, NSA_HEADS * NSA_DH), w_out, xs, gain(layer, 3))
            odd_s.append((kv4(ck, nseq, 1), kv4(cv, nseq, 1), kv4(sk, nseq, 1), kv4(sv, nseq, 1),
                          wk_new[0], wv_new[0]))

        w_q = mem_w_q[layer].astype(BF)
        w_o = mem_w_o[layer].astype(BF)
        xp = _cross_sublayer(xp, gain(layer, 4), gain(layer, 5), w_q, w_o, mk_p, mv_p, 0, n, s)
        xs = _cross_sublayer(xs, gain(layer, 4), gain(layer, 5), w_q, w_o, cache_mem_k, cache_mem_v, layer, nseq, 1)
        xp = _ffn(xp, gain(layer, 6), gain(layer, 7), w_gu[1], w_dn[1])
        xs = _ffn(xs, gain(layer, 6), gain(layer, 7), w_gu[1], w_dn[1])

    st = lambda lst, i: jnp.stack([e[i] for e in lst])
    return (xp.reshape(n, s, d), xs.reshape(nseq, 1, d),
            st(even_p, 0), st(even_p, 1), st(even_p, 2), st(even_p, 3),
            st(odd_p, 0), st(odd_p, 1), st(odd_p, 2), st(odd_p, 3), st(odd_p, 4), st(odd_p, 5),
            st(mem_p, 0), st(mem_p, 1),
            st(even_s, 0), st(even_s, 1), st(even_s, 2), st(even_s, 3),
            st(odd_s, 0), st(odd_s, 1), st(odd_s, 2), st(odd_s, 3), st(odd_s, 4), st(odd_s, 5))
```

```python
import functools
import math

import jax
import jax.numpy as jnp
from jax import lax
from jax.experimental import pallas as pl
from jax.experimental.pallas import tpu as pltpu

BF = jnp.bfloat16
F32 = jnp.float32
SDS = jax.ShapeDtypeStruct

MLA_HEADS, MLA_NOPE, MLA_ROPE, MLA_DV = 12, 64, 32, 64
Q_LORA, KV_LORA = 384, 256
DIFF_HEADS, DIFF_DH, DIFF_DV = 4, 32, 64
NSA_HEADS, NSA_KV_HEADS, NSA_DH = 16, 2, 64
NSA_GROUP = NSA_HEADS // NSA_KV_HEADS
CMP_STRIDE, CMP_BLOCK, CMP_HID = 16, 32, 128
SEL_BLOCK, N_SEL, WINDOW = 64, 15, 512
MEM_HEADS, MEM_DH = 4, 128
HALF_STEP = 0.5
ROPE_THETA = 10000.0
RMS_EPS = 1e-6
NEG = -1e30
TINY = 1e-30
LOWEST = -3e38
MLA_SCALE = (MLA_NOPE + MLA_ROPE) ** -0.5
DIFF_SCALE = DIFF_DH ** -0.5
NSA_SCALE = NSA_DH ** -0.5
MEM_SCALE = MEM_DH ** -0.5
LOG2E = math.log2(math.e)
LANES = 128
EVEN_DECODE_PAGES_PER_STEP = 32
CMP_DECODE_PAGES_PER_STEP = 32
NSA_TQ, NSA_TK_SEL = 256, 1024
DIFF_TQ, DIFF_TK = 256, 1024
MLA_TQ, MLA_TK = 256, 1024
FFN_TF = 1408


def _dot(a, b):
    return jnp.dot(a, b, preferred_element_type=F32)


def _dot_nt(a, b):
    return lax.dot_general(a, b, (((1,), (1,)), ((), ())), preferred_element_type=F32)


def _dot_x3(x, w):
    h1 = x.astype(BF)
    r1 = x - h1.astype(F32)
    h2 = r1.astype(BF)
    h3 = (r1 - h2.astype(F32)).astype(BF)
    return _dot(h1, w) + _dot(h2, w) + _dot(h3, w)


def _rms(x, g):
    return x * lax.rsqrt(jnp.mean(x * x, axis=-1, keepdims=True) + RMS_EPS) * g


def _masked_softmax(s, mask):
    s = jnp.where(mask, s, NEG)
    e = jnp.where(mask, jnp.exp(s - s.max(-1, keepdims=True)), 0.0)
    return e / jnp.maximum(e.sum(-1, keepdims=True), TINY)


def _tile(m, pref):
    if m <= pref:
        return m
    t = pref
    while m % t:
        t //= 2
    return t


def _full(shape):
    n = len(shape)
    return pl.BlockSpec(shape, lambda *a: (0,) * n)


def _ffn_body(x_ref, gpre_ref, gpost_ref, wg_ref, wu_ref, wd_ref, o_ref, xn_ref, acc_ref, *, nf):
    j = pl.program_id(1)

    @pl.when(j == 0)
    def _():
        xn_ref[...] = _rms(x_ref[...], gpre_ref[...]).astype(BF)
        acc_ref[...] = jnp.zeros_like(acc_ref)

    xn = xn_ref[...]
    g = _dot(xn, wg_ref[...])
    u = _dot(xn, wu_ref[...])
    h = (jax.nn.silu(g) * u).astype(BF)
    acc_ref[...] += _dot(h, wd_ref[...])

    @pl.when(j == nf - 1)
    def _():
        o_ref[...] = x_ref[...] + HALF_STEP * _rms(acc_ref[...], gpost_ref[...])


def _ffn(x, gpre, gpost, w_gu, w_down):
    m, d = x.shape
    f = w_down.shape[0]
    tf = FFN_TF if f % FFN_TF == 0 else 256
    nf = f // tf
    tm = _tile(m, 1024)
    return pl.pallas_call(
        functools.partial(_ffn_body, nf=nf),
        grid=(m // tm, nf),
        in_specs=[
            pl.BlockSpec((tm, d), lambda i, j: (i, 0)),
            pl.BlockSpec((1, d), lambda i, j: (0, 0)),
            pl.BlockSpec((1, d), lambda i, j: (0, 0)),
            pl.BlockSpec((d, tf), lambda i, j: (0, j)),
            pl.BlockSpec((d, tf), lambda i, j: (0, j + nf)),
            pl.BlockSpec((tf, d), lambda i, j: (j, 0)),
        ],
        out_specs=pl.BlockSpec((tm, d), lambda i, j: (i, 0)),
        out_shape=SDS((m, d), F32),
        scratch_shapes=[pltpu.VMEM((tm, d), BF), pltpu.VMEM((tm, d), F32)],
        name="ffn",
    )(x, gpre, gpost, w_gu, w_gu, w_down)


def _norm_matmul_body(x_ref, g_ref, w_ref, o_ref, xn_ref):
    @pl.when(pl.program_id(1) == 0)
    def _():
        xn_ref[...] = _rms(x_ref[...], g_ref[...]).astype(BF)

    o_ref[...] = _dot(xn_ref[...], w_ref[...]).astype(o_ref.dtype)


def _norm_matmul(x, g, w, out_dtype):
    m, d = x.shape
    n = w.shape[1]
    tm = _tile(m, 512)
    tn = _tile(n, 512)
    return pl.pallas_call(
        _norm_matmul_body,
        grid=(m // tm, n // tn),
        in_specs=[
            pl.BlockSpec((tm, d), lambda i, j: (i, 0)),
            pl.BlockSpec((1, d), lambda i, j: (0, 0)),
            pl.BlockSpec((d, tn), lambda i, j: (0, j)),
        ],
        out_specs=pl.BlockSpec((tm, tn), lambda i, j: (i, j)),
        out_shape=SDS((m, n), out_dtype),
        scratch_shapes=[pltpu.VMEM((tm, d), BF)],
        name="norm_matmul",
    )(x, g, w)


def _matmul_norm_res_body(a_ref, w_ref, x_ref, g_ref, o_ref):
    y = _dot(a_ref[...].astype(BF), w_ref[...])
    o_ref[...] = x_ref[...] + _rms(y, g_ref[...])


def _matmul_norm_res(a, w, x, g):
    m, k = a.shape
    d = w.shape[1]
    tm = _tile(m, 512)
    return pl.pallas_call(
        _matmul_norm_res_body,
        grid=(m // tm,),
        in_specs=[
            pl.BlockSpec((tm, k), lambda i: (i, 0)),
            pl.BlockSpec((k, d), lambda i: (0, 0)),
            pl.BlockSpec((tm, d), lambda i: (i, 0)),
            pl.BlockSpec((1, d), lambda i: (0, 0)),
        ],
        out_specs=pl.BlockSpec((tm, d), lambda i: (i, 0)),
        out_shape=SDS((m, d), F32),
        name="matmul_norm_res",
    )(a, w, x, g)


def _batched_matmul_body(a_ref, w_ref, o_ref):
    o_ref[0] = _dot(a_ref[0].astype(BF), w_ref[0]).astype(o_ref.dtype)


def _batched_matmul(a, w, out_dtype):
    b, m, k = a.shape
    n = w.shape[2]
    return pl.pallas_call(
        _batched_matmul_body,
        grid=(b,),
        in_specs=[pl.BlockSpec((1, m, k), lambda i: (i, 0, 0)), pl.BlockSpec((1, k, n), lambda i: (i, 0, 0))],
        out_specs=pl.BlockSpec((1, m, n), lambda i: (i, 0, 0)),
        out_shape=SDS((b, m, n), out_dtype),
        name="batched_matmul",
    )(a, w)


def _rope_tables(pos, d):
    half = d // 2
    inv = ROPE_THETA ** (-jnp.arange(half, dtype=F32) * (2.0 / d))
    ang = pos.astype(F32)[:, None] * inv[None, :]
    c, s = jnp.cos(ang), jnp.sin(ang)
    return jnp.concatenate([c, c], -1), jnp.concatenate([s, s], -1)


def _rot_cols(w, d):
    k, n = w.shape
    wr = w.reshape(k, n // d, 2, d // 2)
    return jnp.concatenate([-wr[:, :, 1], wr[:, :, 0]], axis=2).reshape(k, n)


def _pad_cols(w, n):
    return jnp.pad(w, ((0, 0), (0, n - w.shape[1])))


E_CQ, E_CKV, E_DV, E_DQ, E_DK, E_KR = 0, 384, 640, 896, 1152, 1408
E_DQR, E_DKR, E_KRR, E_W = 1536, 1792, 2048, 2176
QW = MLA_HEADS * LANES


def _even_weights(w_in, w_uq, w_uk, w_uv):
    dqk = DIFF_HEADS * 2 * DIFF_DH
    o = [0, Q_LORA, Q_LORA + KV_LORA, Q_LORA + KV_LORA + MLA_ROPE]
    c_q, c_kv, k_r = w_in[:, o[0]:o[1]], w_in[:, o[1]:o[2]], w_in[:, o[2]:o[3]]
    dq = w_in[:, o[3]:o[3] + dqk]
    dk = w_in[:, o[3] + dqk:o[3] + 2 * dqk]
    dv = w_in[:, o[3] + 2 * dqk:]
    w1 = jnp.concatenate([c_q, c_kv, dv, dq, dk, _pad_cols(k_r, LANES), _rot_cols(dq, DIFF_DH),
                          _rot_cols(dk, DIFF_DH), _pad_cols(_rot_cols(k_r, MLA_ROPE), LANES)], axis=1).astype(BF)
    uq = w_uq.reshape(Q_LORA, MLA_HEADS, MLA_NOPE + MLA_ROPE)
    z = jnp.zeros((Q_LORA, MLA_HEADS, LANES - MLA_NOPE - MLA_ROPE), w_uq.dtype)
    wq = jnp.concatenate([uq, z], axis=2).reshape(Q_LORA, QW).astype(BF)
    rot = _rot_cols(uq[:, :, MLA_NOPE:].reshape(Q_LORA, -1), MLA_ROPE).reshape(Q_LORA, MLA_HEADS, MLA_ROPE)
    wqr = jnp.concatenate([jnp.zeros_like(uq[:, :, :MLA_NOPE]), rot, z], axis=2).reshape(Q_LORA, QW).astype(BF)
    uk = w_uk.reshape(KV_LORA, MLA_HEADS, MLA_NOPE)
    wk_lat = jnp.concatenate([uk, jnp.zeros((KV_LORA, MLA_HEADS, LANES - MLA_NOPE), w_uk.dtype)], axis=2)
    wk_lat = wk_lat.reshape(KV_LORA, QW).astype(BF)
    place = jnp.zeros((LANES, LANES), F32).at[jnp.arange(MLA_ROPE), MLA_NOPE + jnp.arange(MLA_ROPE)].set(1.0)
    wk_rope = jnp.tile(place, (1, MLA_HEADS)).astype(BF)
    wabs = jnp.zeros((MLA_HEADS, LANES, KV_LORA + LANES), F32)
    wabs = wabs.at[:, :MLA_NOPE, :KV_LORA].set(jnp.transpose(uk, (1, 2, 0)))
    wabs = wabs.at[:, MLA_NOPE + jnp.arange(MLA_ROPE), KV_LORA + jnp.arange(MLA_ROPE)].set(1.0)
    wuv3 = jnp.transpose(w_uv.reshape(KV_LORA, MLA_HEADS, MLA_DV), (1, 0, 2)).astype(BF)
    return dict(w1=w1, wq=wq, wqr=wqr, wk_lat=wk_lat, wk_rope=wk_rope, wuv=w_uv.astype(BF),
                wabs=wabs.astype(BF), wuv3=wuv3)


def _even_tables(pos):
    c32, s32 = _rope_tables(pos, MLA_ROPE)
    t = pos.shape[0]
    cos256 = jnp.tile(c32, (1, 8))
    sin256 = jnp.tile(s32, (1, 8))
    cosq = jnp.concatenate([jnp.ones((t, MLA_NOPE), F32), c32, jnp.zeros((t, 32), F32)], axis=1)
    sinq = jnp.concatenate([jnp.zeros((t, MLA_NOPE), F32), s32, jnp.zeros((t, 32), F32)], axis=1)
    return cos256, sin256, cosq, sinq


def _even_proj_body(x_ref, g_ref, qg_ref, kvg_ref, w1_ref, wq_ref, wqr_ref, wkl_ref, wkr_ref, wuv_ref,
                    c256_ref, s256_ref, cq_ref, sq_ref,
                    q_out, k_out, v_out, ckv_out, kr_out, dq_out, dk_out, dkb_out, dv_out, dvb_out):
    xn = _rms(x_ref[...], g_ref[...]).astype(BF)
    proj = _dot(xn, w1_ref[...])
    c256, s256 = c256_ref[...], s256_ref[...]
    dq = proj[:, E_DQ:E_DQ + 256] * c256 + proj[:, E_DQR:E_DQR + 256] * s256
    dk = proj[:, E_DK:E_DK + 256] * c256 + proj[:, E_DKR:E_DKR + 256] * s256
    kr = proj[:, E_KR:E_KR + LANES] * c256[:, :LANES] + proj[:, E_KRR:E_KRR + LANES] * s256[:, :LANES]
    dv = proj[:, E_DV:E_DV + 256]
    cqn = _rms(proj[:, E_CQ:E_CQ + Q_LORA], qg_ref[...]).astype(BF)
    ckv = _rms(proj[:, E_CKV:E_CKV + KV_LORA], kvg_ref[...])
    ckv_b = ckv.astype(BF)
    q = _dot(cqn, wq_ref[...])
    qr = _dot(cqn, wqr_ref[...])
    cq, sq = cq_ref[...], sq_ref[...]
    for h in range(MLA_HEADS):
        sl = slice(h * LANES, (h + 1) * LANES)
        q_out[:, sl] = (q[:, sl] * cq + qr[:, sl] * sq).astype(BF)
    k_out[...] = (_dot(ckv_b, wkl_ref[...]) + _dot(kr.astype(BF), wkr_ref[...])).astype(BF)
    v_out[...] = _dot(ckv_b, wuv_ref[...]).astype(BF)
    ckv_out[...] = ckv
    kr_out[...] = kr
    dq_out[...] = dq.astype(BF)
    dk_out[...] = dk
    dkb_out[...] = dk.astype(BF)
    dv_out[...] = dv
    dvb_out[...] = dv.astype(BF)


def _even_proj(x, g, q_g, kv_g, ew, tables):
    m, d = x.shape
    tm = _tile(m, 256)
    nt = tables[0].shape[0] // tm
    row = lambda w: pl.BlockSpec((tm, w), lambda i: (i, 0))
    tab = lambda w: pl.BlockSpec((tm, w), lambda i: (i % nt, 0))
    vw = MLA_HEADS * MLA_DV
    return pl.pallas_call(
        _even_proj_body,
        grid=(m // tm,),
        in_specs=[row(d), _full((1, d)), _full((1, Q_LORA)), _full((1, KV_LORA)), _full((d, E_W)),
                  _full((Q_LORA, QW)), _full((Q_LORA, QW)), _full((KV_LORA, QW)), _full((LANES, QW)),
                  _full((KV_LORA, vw)), tab(256), tab(256), tab(LANES), tab(LANES)],
        out_specs=[row(QW), row(QW), row(vw), row(KV_LORA), row(LANES), row(256), row(256), row(256), row(256),
                   row(256)],
        out_shape=[SDS((m, QW), BF), SDS((m, QW), BF), SDS((m, vw), BF), SDS((m, KV_LORA), F32),
                   SDS((m, LANES), F32), SDS((m, 256), BF), SDS((m, 256), F32), SDS((m, 256), BF),
                   SDS((m, 256), F32), SDS((m, 256), BF)],
        name="even_proj",
    )(x, g, q_g, kv_g, ew["w1"], ew["wq"], ew["wqr"], ew["wk_lat"], ew["wk_rope"], ew["wuv"], *tables)


def _mla_prompt_body(q_ref, k_ref, v_ref, o_ref, m_ref, l_ref, acc_ref, *, tq, tk):
    q0 = pl.program_id(2) * tq
    lo_half = lax.broadcasted_iota(jnp.int32, (tq, LANES), 1) < MLA_DV
    m_ref[...] = jnp.full_like(m_ref, NEG)
    l_ref[...] = jnp.zeros_like(l_ref)
    acc_ref[...] = jnp.zeros_like(acc_ref)

    def tile(j, masked):
        k0 = pl.multiple_of(j * tk, tk)
        vt = v_ref[pl.ds(k0, tk), :]
        if masked:
            mask = (k0 + lax.broadcasted_iota(jnp.int32, (1, tk), 1)) <= (q0 + lax.broadcasted_iota(jnp.int32, (tq, 1), 0))
        for a in range(2):
            sl = slice(a * LANES, (a + 1) * LANES)
            s = _dot_nt(q_ref[:, sl], k_ref[pl.ds(k0, tk), sl]) * (MLA_SCALE * LOG2E)
            if masked:
                s = jnp.where(mask, s, NEG)
            m_old = m_ref[a]
            m_new = jnp.maximum(m_old, s.max(-1, keepdims=True))
            corr = jnp.exp2(m_old - m_new)
            p = jnp.exp2(s - m_new)
            l_ref[a] = l_ref[a] * corr + p.sum(-1, keepdims=True)
            m_ref[a] = m_new
            acc_ref[a] = acc_ref[a] * corr + _dot(p.astype(BF), vt)

    def full_tile(j, carry):
        tile(j, False)
        return carry

    jd = q0 // tk
    lax.fori_loop(0, jd, full_tile, 0)
    tile(jd, True)
    o_ref[...] = jnp.where(lo_half, acc_ref[0] / l_ref[0], acc_ref[1] / l_ref[1]).astype(o_ref.dtype)


def _mla_prompt(q, k, v, n, s):
    tq = _tile(s, MLA_TQ)
    tk = _tile(s, MLA_TK)
    assert tk % tq == 0
    nq = s // tq
    pairs = MLA_HEADS // 2
    return pl.pallas_call(
        functools.partial(_mla_prompt_body, tq=tq, tk=tk),
        grid=(n, pairs, nq),
        in_specs=[
            pl.BlockSpec((tq, 2 * LANES), lambda b, p, i: (b * nq + i, p)),
            pl.BlockSpec((s, 2 * LANES), lambda b, p, i: (b, p)),
            pl.BlockSpec((s, LANES), lambda b, p, i: (b, p)),
        ],
        out_specs=pl.BlockSpec((tq, LANES), lambda b, p, i: (b * nq + i, p)),
        out_shape=SDS((n * s, MLA_HEADS * MLA_DV), BF),
        scratch_shapes=[pltpu.VMEM((2, tq, 1), F32), pltpu.VMEM((2, tq, 1), F32), pltpu.VMEM((2, tq, LANES), F32)],
        name="mla_prompt",
    )(q, k, v)


def _diff_prompt_body(q_ref, k_ref, v_ref, o0_ref, o1_ref, m_ref, l_ref, acc_ref, *, tq, tk):
    q0 = pl.program_id(1) * tq
    dw = DIFF_HEADS * DIFF_DV
    lane = lax.broadcasted_iota(jnp.int32, (tq, dw), 1)
    lo_half = lax.broadcasted_iota(jnp.int32, (tq, LANES), 1) < DIFF_DV
    nmaps = 2 * DIFF_HEADS
    q = q_ref[...]
    qms = [jnp.where((lane >= idx * DIFF_DH) & (lane < (idx + 1) * DIFF_DH), q, jnp.zeros_like(q))
           for idx in range(nmaps)]
    m_ref[...] = jnp.full_like(m_ref, NEG)
    l_ref[...] = jnp.zeros_like(l_ref)
    acc_ref[...] = jnp.zeros_like(acc_ref)

    def tile(j, masked):
        k0 = pl.multiple_of(j * tk, tk)
        kt = k_ref[pl.ds(k0, tk), :]
        if masked:
            mask = (k0 + lax.broadcasted_iota(jnp.int32, (1, tk), 1)) <= (q0 + lax.broadcasted_iota(jnp.int32, (tq, 1), 0))
        for idx in range(nmaps):
            h = idx // 2
            s = _dot_nt(qms[idx], kt) * (DIFF_SCALE * LOG2E)
            if masked:
                s = jnp.where(mask, s, NEG)
            m_old = m_ref[idx]
            m_new = jnp.maximum(m_old, s.max(-1, keepdims=True))
            corr = jnp.exp2(m_old - m_new)
            p = jnp.exp2(s - m_new)
            l_ref[idx] = l_ref[idx] * corr + p.sum(-1, keepdims=True)
            m_ref[idx] = m_new
            vt = v_ref[pl.ds(k0, tk), (h // 2) * LANES:(h // 2 + 1) * LANES]
            acc_ref[idx] = acc_ref[idx] * corr + _dot(p.astype(BF), vt)

    def full_tile(j, carry):
        tile(j, False)
        return carry

    jd = q0 // tk
    lax.fori_loop(0, jd, full_tile, 0)
    tile(jd, True)
    for mm, o_ref in ((0, o0_ref), (1, o1_ref)):
        for pb in range(DIFF_HEADS // 2):
            a, b = (2 * pb) * 2 + mm, (2 * pb + 1) * 2 + mm
            o_ref[:, pb * LANES:(pb + 1) * LANES] = jnp.where(lo_half, acc_ref[a] / l_ref[a], acc_ref[b] / l_ref[b])


def _diff_prompt(dq, dk, dv, n, s):
    tq = _tile(s, DIFF_TQ)
    tk = _tile(s, DIFF_TK)
    assert tk % tq == 0
    nq = s // tq
    nmaps = 2 * DIFF_HEADS
    dw = DIFF_HEADS * DIFF_DV
    row = pl.BlockSpec((tq, dw), lambda b, i: (b * nq + i, 0))
    seq = pl.BlockSpec((s, dw), lambda b, i: (b, 0))
    return pl.pallas_call(
        functools.partial(_diff_prompt_body, tq=tq, tk=tk),
        grid=(n, nq),
        in_specs=[row, seq, seq],
        out_specs=[row, row],
        out_shape=[SDS((n * s, dw), F32), SDS((n * s, dw), F32)],
        scratch_shapes=[pltpu.VMEM((nmaps, tq, 1), F32), pltpu.VMEM((nmaps, tq, 1), F32),
                        pltpu.VMEM((nmaps, tq, LANES), F32)],
        name="diff_prompt",
    )(dq, dk, dv)


def _even_finish_body(om_ref, o0_ref, o1_ref, lam_ref, dg_ref, wm_ref, wd_ref, x_ref, g_ref, o_ref, *, lam_init):
    lp = lam_ref[...]
    lam = (jnp.exp(jnp.sum(lp[0:1] * lp[1:2], axis=-1, keepdims=True))
           - jnp.exp(jnp.sum(lp[2:3] * lp[3:4], axis=-1, keepdims=True)) + lam_init)
    d = o0_ref[...] - lam * o1_ref[...]
    tm = d.shape[0]
    lane = lax.broadcasted_iota(jnp.int32, (tm, 256), 1)
    d2 = d * d
    inv = jnp.zeros_like(d)
    for h in range(DIFF_HEADS):
        hmask = (lane >= h * DIFF_DV) & (lane < (h + 1) * DIFF_DV)
        ms = jnp.sum(jnp.where(hmask, d2, 0.0), axis=-1, keepdims=True) * (1.0 / DIFF_DV)
        inv = jnp.where(hmask, lax.rsqrt(ms + RMS_EPS), inv)
    dn = (d * inv * dg_ref[...]) * (1.0 - lam_init)
    y = _dot(om_ref[...], wm_ref[...]) + _dot(dn.astype(BF), wd_ref[...])
    o_ref[...] = x_ref[...] + _rms(y, g_ref[...])


def _even_finish(o_mla, o0, o1, lam_p, diff_g256, w_out, x, g, lam_init):
    m, d = x.shape
    tm = _tile(m, 512)
    vw = MLA_HEADS * MLA_DV
    row = lambda w: pl.BlockSpec((tm, w), lambda i: (i, 0))
    return pl.pallas_call(
        functools.partial(_even_finish_body, lam_init=lam_init),
        grid=(m // tm,),
        in_specs=[row(vw), row(256), row(256), _full((4, DIFF_DH)), _full((1, 256)), _full((vw, d)),
                  _full((256, d)), row(d), _full((1, d))],
        out_specs=row(d),
        out_shape=SDS((m, d), F32),
        name="even_finish",
    )(o_mla, o0, o1, lam_p, diff_g256, w_out[:vw], w_out[vw:], x, g)


def _cross_body(q_ref, mk_ref, mv_ref, o_ref):
    for h in range(MEM_HEADS):
        sl = slice(h * MEM_DH, (h + 1) * MEM_DH)
        k = mk_ref[0, 0, :, h, :].astype(BF)
        v = mv_ref[0, 0, :, h, :].astype(BF)
        s = _dot_nt(q_ref[0, :, sl], k) * MEM_SCALE
        p = jax.nn.softmax(s, axis=-1)
        o_ref[0, :, sl] = _dot(p.astype(BF), v).astype(o_ref.dtype)


def _cross(q, mk, mv, layer):
    n, t, w = q.shape
    mem = mk.shape[2]
    tq = _tile(t, 512)
    mspec = pl.BlockSpec((1, 1, mem, MEM_HEADS, MEM_DH), lambda b, i: (layer, b, 0, 0, 0))
    return pl.pallas_call(
        _cross_body,
        grid=(n, t // tq),
        in_specs=[pl.BlockSpec((1, tq, w), lambda b, i: (b, i, 0)), mspec, mspec],
        out_specs=pl.BlockSpec((1, tq, w), lambda b, i: (b, i, 0)),
        out_shape=SDS((n, t, w), BF),
        name="cross",
    )(q, mk, mv)


def _even_decode_body(pt_ref, qd_ref, dq_ref, knew_ref, dkn_ref, dvn_ref, *rest, gp):
    lat_refs, kr_refs = rest[0:gp], rest[gp:2 * gp]
    dk_refs, dv_refs = rest[2 * gp:3 * gp], rest[3 * gp:4 * gp]
    lat_out, od_out, m1, l1, acc1, m2, l2, acc2 = rest[4 * gp:]
    step = pl.program_id(1)
    qd = qd_ref[0]
    dq = dq_ref[0]

    @pl.when(step == 0)
    def _():
        kn = knew_ref[0].astype(BF)
        s1 = jnp.sum(qd.astype(F32) * kn.astype(F32), axis=-1, keepdims=True) * MLA_SCALE
        m1[...] = s1
        l1[...] = jnp.ones_like(l1)
        acc1[...] = jnp.broadcast_to(kn[:, :KV_LORA].astype(F32), acc1.shape)
        s2 = jnp.sum(dq.astype(F32) * dkn_ref[0].astype(BF).astype(F32), axis=-1, keepdims=True) * DIFF_SCALE
        m2[...] = s2
        l2[...] = jnp.ones_like(l2)
        acc2[...] = jnp.broadcast_to(dvn_ref[0].astype(BF).astype(F32), acc2.shape)

    c = jnp.concatenate([r[0, 0] for r in lat_refs], axis=0).astype(BF)
    krt = jnp.concatenate([r[0, 0] for r in kr_refs], axis=1).astype(BF)
    s = (_dot_nt(qd[:, :KV_LORA], c) + _dot(qd[:, KV_LORA:KV_LORA + MLA_ROPE], krt)) * MLA_SCALE
    m_new = jnp.maximum(m1[...], s.max(-1, keepdims=True))
    corr = jnp.exp(m1[...] - m_new)
    p = jnp.exp(s - m_new)
    l1[...] = l1[...] * corr + p.sum(-1, keepdims=True)
    acc1[...] = acc1[...] * corr + _dot(p.astype(BF), c)
    m1[...] = m_new

    kt = jnp.concatenate([r[0, 0] for r in dk_refs], axis=1).astype(BF)
    vt = jnp.concatenate([r[0, 0] for r in dv_refs], axis=1).astype(BF)
    s = _dot(dq, kt) * DIFF_SCALE
    m_new = jnp.maximum(m2[...], s.max(-1, keepdims=True))
    corr = jnp.exp(m2[...] - m_new)
    p = jnp.exp(s - m_new)
    l2[...] = l2[...] * corr + p.sum(-1, keepdims=True)
    acc2[...] = acc2[...] * corr + _dot_nt(p.astype(BF), vt)
    m2[...] = m_new

    @pl.when(step == pl.num_programs(1) - 1)
    def _():
        lat_out[0] = acc1[...] / l1[...]
        od_out[0] = acc2[...] / l2[...]


def _paged_specs(shape_tail, li, gp):
    nd = len(shape_tail)
    specs = []
    for t in range(gp):
        specs.append(pl.BlockSpec((1, 1) + shape_tail,
                                  lambda n, s, pt, t=t: (li, pt[n, s * gp + t]) + (0,) * nd))
    return specs


def _feature_major(cache):
    l, pool, rows = cache.shape[:3]
    nd = cache.ndim
    return jnp.transpose(cache, (0, 1) + tuple(range(3, nd)) + (2,)).reshape(l, pool, -1, rows)


def _even_decode(page_table, li, qd, dqb, knew, dkn, dvn, cache_lat, cache_kr, cache_dk, cache_dv):
    nseq, npages = page_table.shape
    gp = math.gcd(EVEN_DECODE_PAGES_PER_STEP, npages)
    page = cache_lat.shape[2]
    hp = qd.shape[1]
    nmaps = 2 * DIFF_HEADS
    dw = DIFF_HEADS * DIFF_DV
    krt, dkt, dvt = _feature_major(cache_kr), _feature_major(cache_dk), _feature_major(cache_dv)
    per_seq = lambda shp: pl.BlockSpec((1,) + shp, lambda n, s, pt: (n, 0, 0))
    in_specs = [per_seq((hp, KV_LORA + LANES)), per_seq((nmaps, dw)), per_seq((1, KV_LORA + LANES)),
                per_seq((1, dw)), per_seq((1, dw))]
    in_specs += _paged_specs((page, KV_LORA), li, gp) + _paged_specs((MLA_ROPE, page), li, gp)
    in_specs += _paged_specs((dw, page), li, gp) * 2
    return pl.pallas_call(
        functools.partial(_even_decode_body, gp=gp),
        grid_spec=pltpu.PrefetchScalarGridSpec(
            num_scalar_prefetch=1,
            grid=(nseq, npages // gp),
            in_specs=in_specs,
            out_specs=[per_seq((hp, KV_LORA)), per_seq((nmaps, dw))],
            scratch_shapes=[pltpu.VMEM((hp, 1), F32), pltpu.VMEM((hp, 1), F32), pltpu.VMEM((hp, KV_LORA), F32),
                            pltpu.VMEM((nmaps, 1), F32), pltpu.VMEM((nmaps, 1), F32),
                            pltpu.VMEM((nmaps, dw), F32)],
        ),
        out_shape=[SDS((nseq, hp, KV_LORA), F32), SDS((nseq, nmaps, dw), F32)],
        name="even_decode",
    )(page_table, qd, dqb, knew, dkn, dvn, *([cache_lat] * gp), *([krt] * gp), *([dkt] * gp), *([dvt] * gp))


O_Q, O_QR, O_CK, O_CV, O_SK, O_SKR, O_SV, O_WK, O_WKR, O_WV, O_GT = (
    0, 1024, 2048, 2176, 2304, 2432, 2560, 2688, 2816, 2944, 3072)
O_SKD, O_SKDR, O_SVD, O_WKD, O_WKDR, O_WVD, O_W = 3200, 3456, 3712, 3968, 4224, 4480, 4736
KVW = NSA_KV_HEADS * NSA_DH


def _dup_cols(w):
    k = w.shape[0]
    wr = w.reshape(k, NSA_KV_HEADS, 1, NSA_DH)
    return jnp.broadcast_to(wr, (k, NSA_KV_HEADS, 2, NSA_DH)).reshape(k, 2 * KVW)


def _odd_weights(w_in):
    hw = NSA_HEADS * NSA_DH
    q = w_in[:, :hw]
    ck, cv, sk, sv, wk, wv = [w_in[:, hw + t * KVW: hw + (t + 1) * KVW] for t in range(6)]
    gl = w_in[:, hw + 6 * KVW:]
    skr, wkr = _rot_cols(sk, NSA_DH), _rot_cols(wk, NSA_DH)
    return jnp.concatenate([q, _rot_cols(q, NSA_DH), ck, cv, sk, skr, sv, wk, wkr, wv, _pad_cols(gl, LANES),
                            _dup_cols(sk), _dup_cols(skr), _dup_cols(sv), _dup_cols(wk), _dup_cols(wkr),
                            _dup_cols(wv)], axis=1).astype(BF)


def _odd_proj_body(x_ref, g_ref, w_ref, cos_ref, sin_ref,
                   q_out, qr_out, ck_out, cv_out, sk_out, sv_out, wk_out, wv_out, gt_out,
                   skd_out, svd_out, wkd_out, wvd_out):
    xn = _rms(x_ref[...], g_ref[...]).astype(BF)
    proj = _dot(xn, w_ref[...])
    cos, sin = cos_ref[...], sin_ref[...]
    for c in range(NSA_HEADS * NSA_DH // LANES):
        sl = slice(c * LANES, (c + 1) * LANES)
        q = proj[:, O_Q + c * LANES:O_Q + (c + 1) * LANES]
        q_out[:, sl] = q.astype(BF)
        qr_out[:, sl] = (q * cos + proj[:, O_QR + c * LANES:O_QR + (c + 1) * LANES] * sin).astype(BF)
    ck_out[...] = proj[:, O_CK:O_CK + KVW]
    cv_out[...] = proj[:, O_CV:O_CV + KVW]
    sk_out[...] = proj[:, O_SK:O_SK + KVW] * cos + proj[:, O_SKR:O_SKR + KVW] * sin
    sv_out[...] = proj[:, O_SV:O_SV + KVW]
    wk_out[...] = proj[:, O_WK:O_WK + KVW] * cos + proj[:, O_WKR:O_WKR + KVW] * sin
    wv_out[...] = proj[:, O_WV:O_WV + KVW]
    gt_out[...] = proj[:, O_GT:O_GT + LANES]
    for g in range(NSA_KV_HEADS):
        sl = slice(g * LANES, (g + 1) * LANES)
        skd_out[:, sl] = (proj[:, O_SKD + g * LANES:O_SKD + (g + 1) * LANES] * cos
                          + proj[:, O_SKDR + g * LANES:O_SKDR + (g + 1) * LANES] * sin).astype(BF)
        wkd_out[:, sl] = (proj[:, O_WKD + g * LANES:O_WKD + (g + 1) * LANES] * cos
                          + proj[:, O_WKDR + g * LANES:O_WKDR + (g + 1) * LANES] * sin).astype(BF)
    svd_out[...] = proj[:, O_SVD:O_SVD + 2 * KVW].astype(BF)
    wvd_out[...] = proj[:, O_WVD:O_WVD + 2 * KVW].astype(BF)


def _odd_proj(x, g, w1, cos, sin):
    m, d = x.shape
    tm = _tile(m, 256)
    nt = cos.shape[0] // tm
    hw = NSA_HEADS * NSA_DH
    row = lambda w: pl.BlockSpec((tm, w), lambda i: (i, 0))
    tab = pl.BlockSpec((tm, LANES), lambda i: (i % nt, 0))
    return pl.pallas_call(
        _odd_proj_body,
        grid=(m // tm,),
        in_specs=[row(d), _full((1, d)), _full((d, O_W)), tab, tab],
        out_specs=[row(hw), row(hw)] + [row(KVW)] * 6 + [row(LANES)] + [row(2 * KVW)] * 4,
        out_shape=[SDS((m, hw), BF), SDS((m, hw), BF)] + [SDS((m, KVW), F32)] * 6 + [SDS((m, LANES), F32)]
        + [SDS((m, 2 * KVW), BF)] * 4,
        name="odd_proj",
    )(x, g, w1, cos, sin)


def _cmp_mlp_body(ch_ref, w1_ref, bias_ref, w2_ref, o_ref):
    ab = _dot(ch_ref[0, 0], w1_ref[0])
    nc = ab.shape[0]
    a = ab[:, :CMP_HID]
    b = pltpu.roll(ab[:, CMP_HID:], nc - 1, 0)
    hid = jax.nn.silu(a + b + bias_ref[0])
    o_ref[0, 0] = _dot(hid.astype(BF), w2_ref[0]).astype(o_ref.dtype)


def _cmp_mlp(ch, w1ab, bias, w2dup):
    _, b, nc, kk = ch.shape
    return pl.pallas_call(
        _cmp_mlp_body,
        grid=(2, b),
        in_specs=[pl.BlockSpec((1, 1, nc, kk), lambda t, i: (t, i, 0, 0)),
                  pl.BlockSpec((1, kk, 2 * CMP_HID), lambda t, i: (t, 0, 0)),
                  pl.BlockSpec((1, 1, CMP_HID), lambda t, i: (t, 0, 0)),
                  pl.BlockSpec((1, CMP_HID, LANES), lambda t, i: (t, 0, 0))],
        out_specs=pl.BlockSpec((1, 1, nc, LANES), lambda t, i: (t, i, 0, 0)),
        out_shape=SDS((2, b, nc, LANES), BF),
        name="cmp_mlp",
    )(ch, w1ab, bias, w2dup)


def _topk_pick(vals, k, width):
    rows = vals.shape[0]
    lane_f = lax.broadcasted_iota(jnp.int32, (rows, width), 1).astype(F32)
    picked = jnp.zeros((rows, width), jnp.bool_)
    idxs = []
    for _ in range(k):
        mx = vals.max(-1, keepdims=True)
        idx = jnp.min(jnp.where(vals == mx, lane_f, float(width)), axis=-1, keepdims=True)
        hit = lane_f == idx
        picked = jnp.logical_or(picked, hit)
        vals = jnp.where(hit, LOWEST, vals)
        idxs.append(idx)
    return picked, idxs


def _nsa_prompt_body(q_ref, qr_ref, gt_ref, kc_ref, vc_ref, ks_ref, vs_ref, kw_ref, vw_ref, e_ref, mb_ref, o_ref,
                     res_ref, m_ref, l_ref, acc_ref, *, tq, tks, tkw, nc, nsb):
    i = pl.program_id(1)
    q0 = i * tq
    qpos = q0 + lax.broadcasted_iota(jnp.int32, (tq, 1), 0)
    lane = lax.broadcasted_iota(jnp.int32, (tq, LANES), 1)
    lo_half = lane < NSA_DH
    gates = jax.nn.sigmoid(gt_ref[...])
    c_end = lax.broadcasted_iota(jnp.int32, (tq, nc), 1) * CMP_STRIDE + (CMP_BLOCK - 1)
    mask_c = c_end <= qpos
    zero_b = jnp.zeros((tq, LANES), BF)
    R = NSA_GROUP

    def stack_q(ref, g):
        parts = []
        for r in range(R):
            hd = g * R + r
            qh = ref[:, (hd // 2) * LANES:(hd // 2 + 1) * LANES]
            parts.append(jnp.where(lo_half if hd % 2 == 0 else jnp.logical_not(lo_half), qh, zero_b))
        return jnp.concatenate(parts, axis=0)

    def gate3(g, b):
        return jnp.stack([gates[:, 3 * (g * R + r) + b:3 * (g * R + r) + b + 1] for r in range(R)], axis=0)

    def online(qst, kt, vt, mask):
        for r in range(R):
            s = _dot_nt(qst[r * tq:(r + 1) * tq], kt) * (NSA_SCALE * LOG2E)
            s = jnp.where(mask, s, NEG)
            m_old = m_ref[r]
            m_new = jnp.maximum(m_old, s.max(-1, keepdims=True))
            corr = jnp.exp2(m_old - m_new)
            p = jnp.exp2(s - m_new)
            l_ref[r] = l_ref[r] * corr + p.sum(-1, keepdims=True)
            acc_ref[r] = acc_ref[r] * corr + _dot(p.astype(BF), vt)
            m_ref[r] = m_new

    def reset():
        m_ref[...] = jnp.full_like(m_ref, NEG)
        l_ref[...] = jnp.zeros_like(l_ref)
        acc_ref[...] = jnp.zeros_like(acc_ref)

    for g in range(NSA_KV_HEADS):
        gsl = slice(g * LANES, (g + 1) * LANES)
        hsl = slice(g * R, (g + 1) * R)
        kc, vc = kc_ref[0, g], vc_ref[0, g]
        s_c = (_dot_nt(stack_q(q_ref, g), kc) * NSA_SCALE).reshape(R, tq, nc)
        p = _masked_softmax(s_c, mask_c[None])
        imp = p.sum(0)
        res_ref[hsl] = gate3(g, 0) * _dot(p.reshape(R * tq, nc).astype(BF), vc).reshape(R, tq, LANES)
        imp_blk = _dot_x3(imp, mb_ref[...])
        cur = qpos // SEL_BLOCK
        vals = jnp.where(lane < cur, imp_blk, NEG)
        vals = jnp.where(lane < nsb, vals, LOWEST)
        picked, _ = _topk_pick(vals, min(N_SEL, nsb), LANES)
        sel = jnp.logical_or(jnp.logical_and(picked, lane < cur), lane == cur)
        selb = jnp.where(sel, 1.0, 0.0).astype(BF)
        qst = stack_q(qr_ref, g)

        reset()

        def sel_tile(j, carry):
            k0 = pl.multiple_of(j * tks, tks)
            kt = ks_ref[pl.ds(k0, tks), gsl]
            vt = vs_ref[pl.ds(k0, tks), gsl]
            kpos = k0 + lax.broadcasted_iota(jnp.int32, (1, tks), 1)
            mask = jnp.logical_and(_dot(selb, e_ref[:, pl.ds(k0, tks)]) > 0.5, kpos <= qpos)
            online(qst, kt, vt, mask)
            return carry

        last_q = q0 + (tq - 1)
        lax.fori_loop(0, last_q // tks + 1, sel_tile, 0)
        res_ref[hsl] += gate3(g, 1) * (acc_ref[...] / jnp.maximum(l_ref[...], TINY))

        k0 = pl.multiple_of(jnp.maximum(q0 + tq - tkw, 0), tq)
        kt = kw_ref[pl.ds(k0, tkw), gsl]
        vt = vw_ref[pl.ds(k0, tkw), gsl]
        kpos = k0 + lax.broadcasted_iota(jnp.int32, (1, tkw), 1)
        mask = jnp.logical_and(kpos <= qpos, kpos > qpos - WINDOW)
        for r in range(R):
            s = jnp.where(mask, _dot_nt(qst[r * tq:(r + 1) * tq], kt) * (NSA_SCALE * LOG2E), NEG)
            p = jnp.exp2(s - s.max(-1, keepdims=True))
            o_w = _dot(p.astype(BF), vt) / p.sum(-1, keepdims=True)
            hd = g * R + r
            res_ref[hd] += gates[:, 3 * hd + 2:3 * hd + 3] * o_w

    for pb in range(NSA_HEADS // 2):
        o_ref[:, pb * LANES:(pb + 1) * LANES] = jnp.where(lo_half, res_ref[2 * pb], res_ref[2 * pb + 1]).astype(
            o_ref.dtype)


def _nsa_tables(s, nc, nsb):
    c = jnp.arange(nc)[:, None]
    j = jnp.arange(LANES)[None, :]
    per = SEL_BLOCK // CMP_STRIDE
    mb = ((c >= per * j - 1) & (c <= per * j + per - 1) & (c < nc - 1) & (j < nsb)).astype(BF)
    e = (jnp.arange(s)[None, :] // SEL_BLOCK == jnp.arange(LANES)[:, None]).astype(BF)
    return mb, e


def _nsa_prompt(q, qr, gt, cmp, skd, svd, wkd, wvd, n, s):
    tq = _tile(s, NSA_TQ)
    tks = _tile(s, NSA_TK_SEL)
    tkw = min(WINDOW + tq, s)
    nc = s // CMP_STRIDE
    nsb = s // SEL_BLOCK
    assert nsb <= LANES
    hw = NSA_HEADS * NSA_DH
    mb, e = _nsa_tables(s, nc, nsb)
    nq = s // tq
    row = lambda w: pl.BlockSpec((tq, w), lambda b, i: (b * nq + i, 0))
    seq = pl.BlockSpec((s, 2 * KVW), lambda b, i: (b, 0))
    return pl.pallas_call(
        functools.partial(_nsa_prompt_body, tq=tq, tks=tks, tkw=tkw, nc=nc, nsb=nsb),
        grid=(n, nq),
        in_specs=[row(hw), row(hw), row(LANES),
                  pl.BlockSpec((1, NSA_KV_HEADS, nc, LANES), lambda b, i: (0, b, 0, 0)),
                  pl.BlockSpec((1, NSA_KV_HEADS, nc, LANES), lambda b, i: (1, b, 0, 0)),
                  seq, seq, seq, seq, _full((LANES, s)), _full((nc, LANES))],
        out_specs=row(hw),
        out_shape=SDS((n * s, hw), BF),
        scratch_shapes=[pltpu.VMEM((NSA_HEADS, tq, LANES), F32), pltpu.VMEM((NSA_GROUP, tq, 1), F32),
                        pltpu.VMEM((NSA_GROUP, tq, 1), F32), pltpu.VMEM((NSA_GROUP, tq, LANES), F32)],
        name="nsa_prompt",
    )(q, qr, gt, cmp, cmp, skd, svd, wkd, wvd, e, mb)


def _nsa_cmp_decode_body(pt_ref, q_ref, wpk_ref, wpv_ref, bias_ref, w2_ref, *rest, gp, nchunks, qpos):
    ck_refs, cv_refs = rest[0:gp], rest[gp:2 * gp]
    oc_out, imp_out, abk, abv, rows_k, rows_v = rest[2 * gp:]
    step = pl.program_id(1)
    page = ck_refs[0].shape[3]
    nr = gp * page // CMP_STRIDE
    hw = 2 * CMP_HID
    for refs, wp, ab, rows in ((ck_refs, wpk_ref, abk, rows_k), (cv_refs, wpv_ref, abv, rows_v)):
        for t, ref in enumerate(refs):
            rows[t * page:(t + 1) * page, :] = ref[0, 0].T
        x = jnp.concatenate([rows[pl.ds(r, nr, stride=CMP_STRIDE), :].astype(BF) for r in range(CMP_STRIDE)], axis=1)
        ab[pl.ds(pl.multiple_of(step * nr, nr), nr), :] = _dot(x, wp[...])

    @pl.when(step == pl.num_programs(1) - 1)
    def _():
        cidx = lax.broadcasted_iota(jnp.int32, (NSA_GROUP, nchunks), 1)
        mask = (cidx < nchunks - 1) & (cidx * CMP_STRIDE + (CMP_BLOCK - 1) <= qpos)
        for g in range(NSA_KV_HEADS):
            kv = []
            for t, ab in enumerate((abk, abv)):
                a = ab[:, g * hw:g * hw + CMP_HID]
                b = pltpu.roll(ab[:, g * hw + CMP_HID:(g + 1) * hw], nchunks - 1, 0)
                hid = jax.nn.silu(a + b + bias_ref[t])
                kv.append(_dot(hid.astype(BF), w2_ref[t]).astype(BF))
            sl = slice(g * NSA_GROUP, (g + 1) * NSA_GROUP)
            p = _masked_softmax(_dot_nt(q_ref[0, sl], kv[0]) * NSA_SCALE, mask)
            oc_out[0, sl] = _dot(p.astype(BF), kv[1])
            imp_out[0, g:g + 1, :] = jnp.sum(p, axis=0, keepdims=True)


def _nsa_cmp_decode(page_table, li, q16, wpk, wpv, bias, w2, cache_ck, cache_cv, qpos):
    nseq, npages = page_table.shape
    gp = math.gcd(CMP_DECODE_PAGES_PER_STEP, npages)
    page = cache_ck.shape[2]
    nchunks = npages * page // CMP_STRIDE
    ckt, cvt = _feature_major(cache_ck), _feature_major(cache_cv)
    per_seq = lambda shp: pl.BlockSpec((1,) + shp, lambda n, s, pt: (n, 0, 0))
    full3 = lambda shp: pl.BlockSpec(shp, lambda n, s, pt: (0, 0, 0))
    wspec = pl.BlockSpec((CMP_STRIDE * KVW, NSA_KV_HEADS * 2 * CMP_HID), lambda n, s, pt: (0, 0))
    in_specs = [per_seq((NSA_HEADS, NSA_DH)), wspec, wspec, full3((2, 1, CMP_HID)),
                full3((2, CMP_HID, NSA_DH))]
    in_specs += _paged_specs((KVW, page), li, gp) * 2
    ab_shape = pltpu.VMEM((nchunks, NSA_KV_HEADS * 2 * CMP_HID), F32)
    rows_shape = pltpu.VMEM((gp * page, KVW), F32)
    return pl.pallas_call(
        functools.partial(_nsa_cmp_decode_body, gp=gp, nchunks=nchunks, qpos=qpos),
        grid_spec=pltpu.PrefetchScalarGridSpec(
            num_scalar_prefetch=1,
            grid=(nseq, npages // gp),
            in_specs=in_specs,
            out_specs=[per_seq((NSA_HEADS, NSA_DH)), per_seq((NSA_KV_HEADS, nchunks))],
            scratch_shapes=[ab_shape, ab_shape, rows_shape, rows_shape],
        ),
        out_shape=[SDS((nseq, NSA_HEADS, NSA_DH), F32), SDS((nseq, NSA_KV_HEADS, nchunks), F32)],
        name="nsa_cmp_decode",
    )(page_table, q16, wpk, wpv, bias, w2, *([ckt] * gp), *([cvt] * gp))


def _imp_topk_body(imp_ref, mb_ref, idx_ref, *, cur, nsb, width):
    imp_blk = _dot_x3(imp_ref[...], mb_ref[...])
    rows = imp_blk.shape[0]
    lane = lax.broadcasted_iota(jnp.int32, (rows, width), 1)
    vals = jnp.where(lane < cur, imp_blk, NEG)
    vals = jnp.where(lane < nsb, vals, LOWEST)
    _, idxs = _topk_pick(vals, min(N_SEL, nsb), width)
    out = jnp.zeros((rows, LANES), F32)
    lane_o = lax.broadcasted_iota(jnp.int32, (rows, LANES), 1)
    for t, idx in enumerate(idxs):
        out = jnp.where(lane_o == t, idx, out)
    idx_ref[...] = out.astype(jnp.int32)


def _imp_topk(imp, nchunks, nsb, cur):
    rows, n_rows = imp.shape
    width = -(-nsb // LANES) * LANES
    per = SEL_BLOCK // CMP_STRIDE
    c = jnp.arange(n_rows)[:, None]
    j = jnp.arange(width)[None, :]
    mb = ((c >= per * j - 1) & (c <= per * j + per - 1) & (c < nchunks - 1) & (j < nsb)).astype(BF)
    return pl.pallas_call(
        functools.partial(_imp_topk_body, cur=cur, nsb=nsb, width=width),
        grid=(1,),
        in_specs=[_full((rows, n_rows)), _full((n_rows, width))],
        out_specs=_full((rows, LANES)),
        out_shape=SDS((rows, LANES), jnp.int32),
        name="imp_topk",
    )(imp, mb)


def _nsa_sel_decode_body(pt_ref, idx_ref, val_ref, qr_ref, gt_ref, oc_ref, skn_ref, svn_ref, wkn_ref, wvn_ref,
                         wkc_ref, wvc_ref, wink_ref, winv_ref, *rest, nslots, wb, bpp):
    ns = nslots * NSA_KV_HEADS
    selk_refs, selv_refs = rest[0:ns], rest[ns:2 * ns]
    o_out, wink_out, winv_out = rest[2 * ns:]
    n = pl.program_id(0)
    gates = jax.nn.sigmoid(gt_ref[0])
    page = selk_refs[0].shape[3]
    width = nslots * page
    lane = lax.broadcasted_iota(jnp.int32, (1, width), 1)
    slot = lane // page
    blk_in_page = (lane % page) // SEL_BLOCK
    for g in range(NSA_KV_HEADS):
        qg = qr_ref[0, g * NSA_GROUP:(g + 1) * NSA_GROUP]
        qf = qg.astype(F32)

        def attend(kt, vt, mask, k_new, v_new):
            s = jnp.where(mask, _dot(qg, kt) * NSA_SCALE, NEG)
            s_new = jnp.sum(qf * k_new.astype(BF).astype(F32), axis=-1, keepdims=True) * NSA_SCALE
            mx = jnp.maximum(s.max(-1, keepdims=True), s_new)
            e = jnp.where(mask, jnp.exp(s - mx), 0.0)
            e_new = jnp.exp(s_new - mx)
            den = jnp.maximum(e.sum(-1, keepdims=True) + e_new, TINY)
            num = _dot_nt(e.astype(BF), vt) + e_new.astype(BF).astype(F32) * v_new.astype(BF).astype(F32)
            return num / den

        kt = jnp.concatenate([selk_refs[g * nslots + t][0, 0] for t in range(nslots)], axis=1).astype(BF)
        vt = jnp.concatenate([selv_refs[g * nslots + t][0, 0] for t in range(nslots)], axis=1).astype(BF)
        want = jnp.zeros((1, width), jnp.int32)
        valid = jnp.zeros((1, width), jnp.int32)
        for t in range(nslots):
            base = (n * NSA_KV_HEADS + g) * nslots + t
            want = jnp.where(slot == t, idx_ref[base] % bpp, want)
            valid = jnp.where(slot == t, val_ref[base], valid)
        o_s = attend(kt, vt, (valid > 0) & (blk_in_page == want), skn_ref[0, g:g + 1], svn_ref[0, g:g + 1])
        dsl = slice(g * NSA_DH, (g + 1) * NSA_DH)
        kw = wink_ref[0, 0, dsl, :].astype(BF)
        vw = winv_ref[0, 0, dsl, :].astype(BF)
        wmask = lax.broadcasted_iota(jnp.int32, (1, wb), 1) > wb - WINDOW
        o_w = attend(kw, vw, wmask, wkn_ref[0, g:g + 1], wvn_ref[0, g:g + 1])
        sl = slice(g * NSA_GROUP, (g + 1) * NSA_GROUP)
        gg = gates[sl]
        o_out[0, sl] = gg[:, 0:1] * oc_ref[0, sl] + gg[:, 1:2] * o_s + gg[:, 2:3] * o_w
    last = lax.broadcasted_iota(jnp.int32, (KVW, wb), 1) == wb - 1
    for win_ref, col_ref, out_ref in ((wink_ref, wkc_ref, wink_out), (winv_ref, wvc_ref, winv_out)):
        out_ref[0, 0] = jnp.where(last, col_ref[0], pltpu.roll(win_ref[0, 0], wb - 1, 1))


def _nsa_sel_decode(page_table, idx, valid, li, qr16, gt, oc, skn, svn, wkn, wvn, win_k, win_v, cache_sk, cache_sv):
    nseq = page_table.shape[0]
    page = cache_sk.shape[2]
    bpp = page // SEL_BLOCK
    nslots = idx.shape[1]
    idx, valid = idx.reshape(-1), valid.reshape(-1)
    wb = win_k.shape[2]
    skt, svt = _feature_major(cache_sk), _feature_major(cache_sv)
    wkt, wvt = _feature_major(win_k), _feature_major(win_v)
    per_seq = lambda shp: pl.BlockSpec((1,) + shp, lambda n, pt, ix, vl: (n, 0, 0))
    wspec = pl.BlockSpec((1, 1, KVW, wb), lambda n, pt, ix, vl: (li, n, 0, 0))
    sel_specs = []
    for g in range(NSA_KV_HEADS):
        for t in range(nslots):
            sel_specs.append(pl.BlockSpec(
                (1, 1, NSA_DH, page),
                lambda n, pt, ix, vl, g=g, t=t: (li, pt[n, ix[(n * NSA_KV_HEADS + g) * nslots + t] // bpp], g, 0)))
    in_specs = [per_seq((NSA_HEADS, NSA_DH)), per_seq((NSA_HEADS, 3)), per_seq((NSA_HEADS, NSA_DH))]
    in_specs += [per_seq((NSA_KV_HEADS, NSA_DH))] * 4 + [per_seq((KVW, 1))] * 2 + [wspec, wspec] + sel_specs * 2
    ns = nslots * NSA_KV_HEADS
    wout = pl.BlockSpec((1, 1, KVW, wb), lambda n, pt, ix, vl: (0, n, 0, 0))
    o, wk_t, wv_t = pl.pallas_call(
        functools.partial(_nsa_sel_decode_body, nslots=nslots, wb=wb, bpp=bpp),
        grid_spec=pltpu.PrefetchScalarGridSpec(
            num_scalar_prefetch=3,
            grid=(nseq,),
            in_specs=in_specs,
            out_specs=[per_seq((NSA_HEADS, NSA_DH)), wout, wout],
        ),
        out_shape=[SDS((nseq, NSA_HEADS, NSA_DH), F32), SDS((1, nseq, KVW, wb), F32), SDS((1, nseq, KVW, wb), F32)],
        name="nsa_sel_decode",
    )(page_table, idx, valid, qr16, gt, oc, skn, svn, wkn, wvn, wkn.reshape(nseq, KVW, 1), wvn.reshape(nseq, KVW, 1),
      wkt, wvt, *([skt] * ns), *([svt] * ns))
    back = lambda z: jnp.transpose(z.reshape(1, nseq, NSA_KV_HEADS, NSA_DH, wb), (0, 1, 4, 2, 3))
    return o, back(wk_t), back(wv_t)


def _cmp_weights(cmp_pos, w1, w2):
    half = CMP_STRIDE * NSA_DH
    w1ab = jnp.concatenate([w1[:, :half], w1[:, half:]], axis=2).astype(BF)
    pos = jnp.zeros((2, 8, 2 * half), F32).at[:, 0].set(cmp_pos.reshape(2, 2 * half))
    bias = _batched_matmul(pos, w1.astype(BF), F32)[:, 0:1]
    return w1ab, bias


def _cross_sublayer(x, g_pre, g_post, w_q, w_o, mk, mv, layer, n, t):
    q = _norm_matmul(x, g_pre, w_q, BF).reshape(n, t, MEM_HEADS * MEM_DH)
    o = _cross(q, mk, mv, layer).reshape(n * t, MEM_HEADS * MEM_DH)
    return _matmul_norm_res(o, w_o, x, g_post)


def kernel(x_prompt, x_sample, cache_mla_latent, cache_mla_krope, cache_diff_k, cache_diff_v, cache_nsa_cmp_k, cache_nsa_cmp_v, cache_nsa_sel_k, cache_nsa_sel_v, state_nsa_win_k, state_nsa_win_v, cache_mem_k, cache_mem_v, page_table, mem_prompt, norm_g, ffn_w_gu, ffn_w_down, mem_norm_g, mem_w_q, mem_w_k, mem_w_v, mem_w_o, even_w_in, even_w_out, mla_q_norm_g, mla_kv_norm_g, mla_w_uq, mla_w_uk, mla_w_uv, diff_lambda, diff_norm_g, odd_w_in, odd_w_out, nsa_cmp_pos, nsa_cmp_w1, nsa_cmp_w2):
    n, s, d = x_prompt.shape
    nseq, t_dec, _ = x_sample.shape
    assert t_dec == 1
    npages = page_table.shape[1]
    page = cache_mla_latent.shape[2]
    past = npages * page
    depth = norm_g.shape[0]
    mem = mem_prompt.shape[1]
    xp = x_prompt.reshape(n * s, d)
    xs = x_sample.reshape(nseq, d)
    pos_p = jnp.arange(s)
    pos_s = jnp.full((nseq,), past, jnp.int32)
    gain = lambda layer, i: norm_g[layer, i][None, :]

    even_p, even_s, odd_p, odd_s, mem_p = [], [], [], [], []
    for layer in range(depth):
        li = layer // 2
        w_gu = ffn_w_gu[layer].astype(BF)
        w_dn = ffn_w_down[layer].astype(BF)
        wkv = jnp.concatenate([mem_w_k[layer], mem_w_v[layer]], axis=1).astype(BF)
        mkv = _norm_matmul(mem_prompt.reshape(n * mem, d), mem_norm_g[layer][None, :], wkv, F32)
        mw = MEM_HEADS * MEM_DH
        mk_p = mkv[:, :mw].reshape(1, n, mem, MEM_HEADS, MEM_DH)
        mv_p = mkv[:, mw:].reshape(1, n, mem, MEM_HEADS, MEM_DH)
        mem_p.append((mk_p[0], mv_p[0]))

        xp = _ffn(xp, gain(layer, 0), gain(layer, 1), w_gu[0], w_dn[0])
        xs = _ffn(xs, gain(layer, 0), gain(layer, 1), w_gu[0], w_dn[0])

        if layer % 2 == 0:
            lam_init = 0.8 - 0.6 * math.exp(-0.3 * layer)
            ew = _even_weights(even_w_in[li], mla_w_uq[li], mla_w_uk[li], mla_w_uv[li])
            w_out = even_w_out[li].astype(BF)
            qg, kvg = mla_q_norm_g[li][None, :], mla_kv_norm_g[li][None, :]
            dg = jnp.tile(diff_norm_g[li], DIFF_HEADS)[None, :]
            (q, k, v, ckv, kr, dq, dk, dkb, dv, dvb) = _even_proj(xp, gain(layer, 2), qg, kvg, ew, _even_tables(pos_p))
            o_mla = _mla_prompt(q, k, v, n, s)
            o0, o1 = _diff_prompt(dq, dkb, dvb, n, s)
            xp = _even_finish(o_mla, o0, o1, diff_lambda[li], dg, w_out, xp, gain(layer, 3), lam_init)
            even_p.append((ckv.reshape(n, s, KV_LORA), kr[:, :MLA_ROPE].reshape(n, s, MLA_ROPE),
                           dk.reshape(n, s, DIFF_HEADS, 2 * DIFF_DH), dv.reshape(n, s, DIFF_HEADS, DIFF_DV)))
            (q, _, _, ckv, kr, dq, dk, _, dv, _) = _even_proj(xs, gain(layer, 2), qg, kvg, ew, _even_tables(pos_s))
            qabs = _batched_matmul(jnp.transpose(q.reshape(nseq, MLA_HEADS, LANES), (1, 0, 2)), ew["wabs"], BF)
            qd = jnp.pad(jnp.transpose(qabs, (1, 0, 2)), ((0, 0), (0, 16 - MLA_HEADS), (0, 0)))
            nmaps = 2 * DIFF_HEADS
            own = (jnp.arange(nmaps * DIFF_DH)[None, :] // DIFF_DH == jnp.arange(nmaps)[:, None]).astype(dq.dtype)
            dqb = dq[:, None, :] * own[None]
            knew = jnp.concatenate([ckv, kr], axis=1)[:, None, :]
            lat, od = _even_decode(page_table, li, qd, dqb, knew, dk[:, None, :], dv[:, None, :], cache_mla_latent,
                                   cache_mla_krope, cache_diff_k, cache_diff_v)
            lat_h = jnp.transpose(lat[:, :MLA_HEADS], (1, 0, 2))
            o_mla = jnp.transpose(_batched_matmul(lat_h, ew["wuv3"], BF), (1, 0, 2)).reshape(nseq, -1)
            od = od.reshape(nseq, DIFF_HEADS, 2, DIFF_HEADS, DIFF_DV)
            o0 = jnp.stack([od[:, h, 0, h] for h in range(DIFF_HEADS)], axis=1).reshape(nseq, -1)
            o1 = jnp.stack([od[:, h, 1, h] for h in range(DIFF_HEADS)], axis=1).reshape(nseq, -1)
            xs = _even_finish(o_mla, o0, o1, diff_lambda[li], dg, w_out, xs, gain(layer, 3), lam_init)
            even_s.append((ckv.reshape(nseq, 1, KV_LORA), kr[:, :MLA_ROPE].reshape(nseq, 1, MLA_ROPE),
                           dk.reshape(nseq, 1, DIFF_HEADS, 2 * DIFF_DH), dv.reshape(nseq, 1, DIFF_HEADS, DIFF_DV)))
        else:
            w1 = _odd_weights(odd_w_in[li])
            w_out = odd_w_out[li].astype(BF)
            w1ab, bias = _cmp_weights(nsa_cmp_pos[li], nsa_cmp_w1[li], nsa_cmp_w2[li])
            w2 = nsa_cmp_w2[li]
            kv4 = lambda z, m_, t_: z.reshape(m_, t_, NSA_KV_HEADS, NSA_DH)
            c64, s64 = _rope_tables(pos_p, NSA_DH)
            (q, qr, ck, cv, sk, sv, wk, wv, gt, skd, svd, wkd, wvd) = _odd_proj(
                xp, gain(layer, 2), w1, jnp.tile(c64, (1, 2)), jnp.tile(s64, (1, 2)))
            nc = s // CMP_STRIDE
            chunks = lambda z: jnp.transpose(z.reshape(n, nc, CMP_STRIDE, NSA_KV_HEADS, NSA_DH), (0, 3, 1, 2, 4)
                                             ).reshape(n * NSA_KV_HEADS, nc, CMP_STRIDE * NSA_DH)
            ch = jnp.stack([chunks(ck), chunks(cv)]).astype(BF)
            cmp = _cmp_mlp(ch, w1ab, bias, jnp.concatenate([w2, w2], axis=2).astype(BF))
            o = _nsa_prompt(q, qr, gt, cmp, skd, svd, wkd, wvd, n, s)
            xp = _matmul_norm_res(o, w_out, xp, gain(layer, 3))
            win = min(WINDOW, s)
            odd_p.append((kv4(ck, n, s), kv4(cv, n, s), kv4(sk, n, s), kv4(sv, n, s),
                          kv4(wk, n, s)[:, -win:], kv4(wv, n, s)[:, -win:]))
            c64, s64 = _rope_tables(pos_s, NSA_DH)
            (q, qr, ck, cv, sk, sv, wk, wv, gt, _, _, _, _) = _odd_proj(
                xs, gain(layer, 2), w1, jnp.tile(c64, (1, 2)), jnp.tile(s64, (1, 2)))
            wab = w1ab.reshape(2, CMP_STRIDE, 1, NSA_DH, 1, 2 * CMP_HID)
            eye = jnp.eye(NSA_KV_HEADS, dtype=wab.dtype)[None, None, :, None, :, None]
            wp = (wab * eye).reshape(2, CMP_STRIDE * KVW, NSA_KV_HEADS * 2 * CMP_HID)
            oc, imp = _nsa_cmp_decode(page_table, li, q.reshape(nseq, NSA_HEADS, NSA_DH), wp[0], wp[1], bias,
                                      w2.astype(BF), cache_nsa_cmp_k, cache_nsa_cmp_v, past)
            nchunks = past // CMP_STRIDE
            nsb = -(-(past + 1) // SEL_BLOCK)
            cur = past // SEL_BLOCK
            ksel = min(N_SEL, nsb)
            top = _imp_topk(imp.reshape(nseq * NSA_KV_HEADS, -1), nchunks, nsb, cur)[:, :ksel]
            valid = (top < cur).astype(jnp.int32)
            idx = jnp.where(top < cur, top, 0)
            gt3 = gt[:, :3 * NSA_HEADS].reshape(nseq, NSA_HEADS, 3)
            g2 = lambda z: z.reshape(nseq, NSA_KV_HEADS, NSA_DH)
            o, wk_new, wv_new = _nsa_sel_decode(
                page_table, idx, valid, li, qr.reshape(nseq, NSA_HEADS, NSA_DH), gt3, oc, g2(sk), g2(sv), g2(wk),
                g2(wv), state_nsa_win_k, state_nsa_win_v, cache_nsa_sel_k, cache_nsa_sel_v)
            xs = _matmul_norm_res(o.reshape(nseq, NSA_HEADS * NSA_DH), w_out, xs, gain(layer, 3))
            odd_s.append((kv4(ck, nseq, 1), kv4(cv, nseq, 1), kv4(sk, nseq, 1), kv4(sv, nseq, 1),
                          wk_new[0], wv_new[0]))

        w_q = mem_w_q[layer].astype(BF)
        w_o = mem_w_o[layer].astype(BF)
        xp = _cross_sublayer(xp, gain(layer, 4), gain(layer, 5), w_q, w_o, mk_p, mv_p, 0, n, s)
        xs = _cross_sublayer(xs, gain(layer, 4), gain(layer, 5), w_q, w_o, cache_mem_k, cache_mem_v, layer, nseq, 1)
        xp = _ffn(xp, gain(layer, 6), gain(layer, 7), w_gu[1], w_dn[1])
        xs = _ffn(xs, gain(layer, 6), gain(layer, 7), w_gu[1], w_dn[1])

    st = lambda lst, i: jnp.stack([e[i] for e in lst])
    return (xp.reshape(n, s, d), xs.reshape(nseq, 1, d),
            st(even_p, 0), st(even_p, 1), st(even_p, 2), st(even_p, 3),
            st(odd_p, 0), st(odd_p, 1), st(odd_p, 2), st(odd_p, 3), st(odd_p, 4), st(odd_p, 5),
            st(mem_p, 0), st(mem_p, 1),
            st(even_s, 0), st(even_s, 1), st(even_s, 2), st(even_s, 3),
            st(odd_s, 0), st(odd_s, 1), st(odd_s, 2), st(odd_s, 3), st(odd_s, 4), st(odd_s, 5))
```
